```python
import math
import jax, jax.numpy as jnp
from jax import lax
import numpy as np

D_MODEL = 1024
BATCH = 32
SEQ = 256
DEPTH = 2
DEC_BATCH = 4
DEC_SEQ = 1024
PAST_LEN = 512

GRID_W = 64
EPS = 1e-6
A_HEADS = 8
A_DIM = 64
A_VDIM = 2 * A_DIM
A_WIDTH = A_HEADS * A_VDIM
ROPE_BASE = 10000.0
Q_BLOCK = 128
B_WIDTH = D_MODEL
CONV_WIDTH = 31
M_HEADS = 4
M_DIM = D_MODEL // M_HEADS
M_WIDTH = M_HEADS * M_DIM
M_CHUNK = 64
N_EXPERTS = 32
TOP_K = 4
D_FF = D_MODEL
SWIGLU_LIMIT = 7.0
SWIGLU_ALPHA = 1.702
MOE_BLOCK = 128
N_MOD = 6
IN_SIZES = (A_HEADS * 2 * A_DIM, A_HEADS * 2 * A_DIM, A_WIDTH, 2 * B_WIDTH,
            M_WIDTH, M_WIDTH, M_WIDTH, M_WIDTH, 2 * M_HEADS, 2 * M_HEADS, 3 * D_MODEL)
IN_DIM = sum(IN_SIZES)
IN_SPLITS = tuple(int(s) for s in np.cumsum(IN_SIZES)[:-1])
F_OFF = sum(IN_SIZES[:9])

kernel_name = 'hybrid_diff_conformer_mlstm_moe_dit_step'


def rms_norm(x, g):
    xf = x.astype(jnp.float32)
    y = xf * lax.rsqrt(jnp.mean(xf * xf, axis=-1, keepdims=True) + EPS)
    return (y * g.astype(jnp.float32)).astype(x.dtype)


def layer_norm(x, g, b):
    xf = x.astype(jnp.float32)
    mu = jnp.mean(xf, axis=-1, keepdims=True)
    xc = xf - mu
    y = xc * lax.rsqrt(jnp.mean(xc * xc, axis=-1, keepdims=True) + EPS)
    return (y * g.astype(jnp.float32) + b.astype(jnp.float32)).astype(x.dtype)


def rope_1d(x, pos):
    half = x.shape[-1] // 2
    freqs = ROPE_BASE ** (-jnp.arange(half, dtype=jnp.float32) / half)
    ang = pos.astype(jnp.float32)[:, None] * freqs
    cos = jnp.cos(ang)[None, :, None, None, :]
    sin = jnp.sin(ang)[None, :, None, None, :]
    xf = x.astype(jnp.float32)
    x1, x2 = xf[..., :half], xf[..., half:]
    return jnp.concatenate([x1 * cos - x2 * sin, x2 * cos + x1 * sin], axis=-1).astype(x.dtype)


def rope_2d(x, row, col):
    half = x.shape[-1] // 2
    return jnp.concatenate([rope_1d(x[..., :half], row), rope_1d(x[..., half:], col)], axis=-1)


def diff_attention(q, k, v, lam, lam_init, subln_g):
    B, Sq = q.shape[0], q.shape[1]
    nq = Sq // Q_BLOCK
    qb = jnp.moveaxis(q.reshape(B, nq, Q_BLOCK, A_HEADS, 2, A_DIM), 1, 0)
    scale = A_DIM ** -0.5

    def block(qi):
        s = jnp.einsum('bqhcd,bkhcd->bchqk', qi, k).astype(jnp.float32) * scale
        a = jax.nn.softmax(s, axis=-1)
        w = a[:, 0] - lam * a[:, 1]
        return jnp.einsum('bhqk,bkhe->bqhe', w.astype(v.dtype), v)

    o = lax.map(block, qb)
    o = jnp.moveaxis(o, 0, 1).reshape(B, Sq, A_HEADS, A_VDIM)
    o = rms_norm(o, subln_g) * (1.0 - lam_init)
    return o.reshape(B, Sq, A_WIDTH)


def conformer_conv(u, w, b, g, beta):
    a, gate = jnp.split(u, 2, axis=-1)
    h = a * jax.nn.sigmoid(gate)
    h = lax.conv_general_dilated(
        h, w[:, None, :].astype(h.dtype), window_strides=(1,),
        padding=[(CONV_WIDTH // 2, CONV_WIDTH // 2)],
        dimension_numbers=('NWC', 'WIO', 'NWC'), feature_group_count=B_WIDTH) + b
    return jax.nn.silu(layer_norm(h, g, beta))


def mlstm_scan(q, k, v, li, lf, C0, n0, m0):
    B, S, H, Dh = q.shape
    nc = S // M_CHUNK

    def chunk(t):
        return jnp.moveaxis(t.reshape((B, nc, M_CHUNK, H) + t.shape[3:]), (1, 3), (0, 2))

    tril = jnp.tril(jnp.ones((M_CHUNK, M_CHUNK), dtype=bool))

    def step(carry, inp):
        C, n, m = carry
        qc, kc, vc, ic, fc = inp
        b = jnp.cumsum(fc, axis=-1)
        dmat = jnp.where(tril, b[..., :, None] - b[..., None, :] + ic[..., None, :], -jnp.inf)
        inter = b + m[..., None]
        mt = jnp.maximum(inter, jnp.max(dmat, axis=-1))
        w = jnp.exp(dmat - mt[..., None])
        s_inter = jnp.exp(inter - mt)
        sw = jnp.einsum('bhtd,bhsd->bhts', qc, kc) * w
        num = s_inter[..., None] * jnp.einsum('bhtd,bhde->bhte', qc, C) + jnp.einsum('bhts,bhse->bhte', sw, vc)
        den = s_inter * jnp.einsum('bhtd,bhd->bht', qc, n) + jnp.sum(sw, axis=-1)
        h = num / jnp.maximum(jnp.abs(den), jnp.exp(-mt))[..., None]
        m_new = mt[..., -1]
        decay = jnp.exp(b[..., -1] + m - m_new)
        wk = jnp.exp(b[..., -1:] - b + ic - m_new[..., None])
        C_new = decay[..., None, None] * C + jnp.einsum('bhs,bhsd,bhse->bhde', wk, kc, vc)
        n_new = decay[..., None] * n + jnp.einsum('bhs,bhsd->bhd', wk, kc)
        return (C_new, n_new, m_new), h

    (C, n, m), hs = lax.scan(step, (C0, n0, m0), (chunk(q), chunk(k), chunk(v), chunk(li), chunk(lf)))
    h = jnp.moveaxis(hs, (0, 2), (1, 3)).reshape(B, S, H, Dh)
    return h, C, n, m


def mlstm_bidir(q, k, v, i_pre, f_pre, C0, n0, m0):
    lf = jax.nn.log_sigmoid(f_pre)
    h_f, Cf, nf, mf = mlstm_scan(q, k, v, i_pre[:, :, 0], lf[:, :, 0], C0[:, 0], n0[:, 0], m0[:, 0])
    flip = lambda t: jnp.flip(t, axis=1)
    h_b, Cb, nb, mb = mlstm_scan(flip(q), flip(k), flip(v), flip(i_pre[:, :, 1]), flip(lf[:, :, 1]),
                                 C0[:, 1], n0[:, 1], m0[:, 1])
    return h_f + flip(h_b), jnp.stack([Cf, Cb], axis=1), jnp.stack([nf, nb], axis=1), jnp.stack([mf, mb], axis=1)


def token_mixer(xn, p, lam_init, ctx):
    B, S, _ = xn.shape
    f32 = jnp.float32
    proj = xn @ p['w_in'] + p['b_in']
    aq, ak, av, bu, mq, mk, mv, mo, mi, mf, gl = jnp.split(proj, IN_SPLITS, axis=-1)
    q = rms_norm(aq.reshape(B, S, A_HEADS, 2, A_DIM), p['qn_g'])
    k = rms_norm(ak.reshape(B, S, A_HEADS, 2, A_DIM), p['kn_g'])
    v = av.reshape(B, S, A_HEADS, A_VDIM)
    lam = (jnp.exp(jnp.sum(p['lam_q1'].astype(f32) * p['lam_k1'].astype(f32)))
           - jnp.exp(jnp.sum(p['lam_q2'].astype(f32) * p['lam_k2'].astype(f32))) + lam_init)
    qm = mq.reshape(B, S, M_HEADS, M_DIM).astype(f32)
    km = mk.reshape(B, S, M_HEADS, M_DIM).astype(f32) * (M_DIM ** -0.5)
    vm = mv.reshape(B, S, M_HEADS, M_DIM).astype(f32)
    i_pre = mi.reshape(B, S, 2, M_HEADS).astype(f32)
    f_pre = mf.reshape(B, S, 2, M_HEADS).astype(f32)
    if ctx is None:
        q_a, k_a, v_a = q, k, v
        C0 = jnp.zeros((B, 2, M_HEADS, M_DIM, M_DIM), f32)
        n0 = jnp.zeros((B, 2, M_HEADS, M_DIM), f32)
        m0 = jnp.zeros((B, 2, M_HEADS), f32)
    else:
        k_ctx, v_ctx, C0, n0, m0 = ctx
        rows = S // GRID_W
        row = jnp.repeat(jnp.arange(rows), GRID_W)
        col = jnp.tile(jnp.arange(GRID_W), rows)
        q_a = rope_2d(q, row, col)
        k_a = jnp.concatenate([rope_2d(k, row, col),
                               k_ctx.reshape(B, -1, A_HEADS, 2, A_DIM).astype(k.dtype)], axis=1)
        v_a = jnp.concatenate([v, v_ctx.astype(v.dtype)], axis=1)
        C0, n0, m0 = C0.astype(f32), n0.astype(f32), m0.astype(f32)
    o_a = diff_attention(q_a, k_a, v_a, lam, lam_init, p['subln_g'])
    o_b = conformer_conv(bu, p['conv_w'], p['conv_b'], p['cln_g'], p['cln_b'])
    h_m, C, n, m = mlstm_bidir(qm, km, vm, i_pre, f_pre, C0, n0, m0)
    o_c = rms_norm(h_m, p['mnorm_g'].reshape(M_HEADS, M_DIM)).astype(xn.dtype).reshape(B, S, M_WIDTH) * jax.nn.sigmoid(mo)
    g_a, g_b, g_c = jnp.split(jax.nn.sigmoid(gl), 3, axis=-1)
    merged = g_a * (o_a @ p['w_br_a']) + g_b * (o_b @ p['w_br_b']) + g_c * (o_c @ p['w_br_c'])
    out = merged @ p['w_out']
    if ctx is None:
        dt = xn.dtype
        return out, (k.reshape(B, S, A_HEADS, 2 * A_DIM), v, C.astype(dt), n.astype(dt), m.astype(dt))
    return out, None


def moe(x, p):
    T, D = x.shape
    logits = (x @ p['router_w'] + p['router_b']).astype(jnp.float32)
    top_v, top_e = lax.top_k(logits, TOP_K)
    gate = jax.nn.softmax(top_v, axis=-1).astype(x.dtype)
    flat_e = top_e.reshape(-1)
    n_pairs = T * TOP_K
    flat_tok = jnp.arange(n_pairs, dtype=jnp.int32) // TOP_K
    order = jnp.argsort(flat_e)
    sorted_e = flat_e[order]
    counts = jnp.bincount(flat_e, length=N_EXPERTS)
    padded = (counts + MOE_BLOCK - 1) // MOE_BLOCK * MOE_BLOCK
    pad_end = jnp.cumsum(padded)
    pad_start = pad_end - padded
    sort_start = jnp.cumsum(counts) - counts
    dest = pad_start[sorted_e] + jnp.arange(n_pairs) - sort_start[sorted_e]
    n_blocks = -(-n_pairs // MOE_BLOCK) + N_EXPERTS
    n_rows = n_blocks * MOE_BLOCK
    row_tok = jnp.zeros((n_rows,), jnp.int32).at[dest].set(flat_tok[order])
    row_gate = jnp.zeros((n_rows,), x.dtype).at[dest].set(gate.reshape(-1)[order])
    block_e = jnp.minimum(jnp.searchsorted(pad_end, jnp.arange(n_blocks) * MOE_BLOCK, side='right'), N_EXPERTS - 1)
    xb = x[row_tok].reshape(n_blocks, MOE_BLOCK, D)

    def expert_block(args):
        xi, e = args
        gu = xi @ p['w_gu'][e] + p['b_gu'][e]
        g, u = jnp.split(gu, 2, axis=-1)
        g = jnp.minimum(g, SWIGLU_LIMIT)
        u = jnp.clip(u, -SWIGLU_LIMIT, SWIGLU_LIMIT)
        hdn = (u + 1.0) * (g * jax.nn.sigmoid(SWIGLU_ALPHA * g))
        return hdn @ p['w_dn'][e] + p['b_dn'][e]

    yb = lax.map(expert_block, (xb, block_e))
    return jax.ops.segment_sum(yb.reshape(n_rows, D) * row_gate[:, None], row_tok, num_segments=T)


def trunk_layer(x, mod, p, lam_init, ctx):
    shift1, scale1, gate1, shift2, scale2, gate2 = jnp.split(mod, N_MOD, axis=-1)
    xn = rms_norm(x, p['norm1_g']) * (1.0 + scale1) + shift1
    h, new = token_mixer(xn, p, lam_init, ctx)
    x = x + gate1 * h
    xn = rms_norm(x, p['norm2_g']) * (1.0 + scale2) + shift2
    B, S, D = x.shape
    x = x + gate2 * moe(xn.reshape(B * S, D), p).reshape(B, S, D)
    return x, new


def setup_inputs(seed: int = 0) -> dict:
    key = jax.random.key(seed)
    ks = iter(jax.random.split(key, 48))
    f32 = jnp.float32
    nrm = lambda shape, s: s * jax.random.normal(next(ks), shape, f32)
    gain = lambda shape: 1.0 + nrm(shape, 0.02)
    L, D = DEPTH, D_MODEL
    b_in = nrm((L, IN_DIM), 0.02)
    f_bias = jnp.tile(jnp.linspace(3.0, 6.0, M_HEADS), 2)
    b_in = b_in.at[:, F_OFF:F_OFF + 2 * M_HEADS].add(f_bias)
    return {
        'x_prompt': nrm((BATCH, SEQ, D), 1.0),
        'x_sample': nrm((DEC_BATCH, DEC_SEQ, D), 1.0),
        'c': nrm((DEC_BATCH, D), 1.0),
        'c_ctx': nrm((D,), 1.0),
        'cache_k': nrm((DEC_BATCH, L, PAST_LEN, A_HEADS, 2 * A_DIM), 1.0),
        'cache_v': nrm((DEC_BATCH, L, PAST_LEN, A_HEADS, A_VDIM), 1.0),
        'state_C': nrm((DEC_BATCH, L, 2, M_HEADS, M_DIM, M_DIM), 0.1),
        'state_n': nrm((DEC_BATCH, L, 2, M_HEADS, M_DIM), 0.5),
        'state_m': nrm((DEC_BATCH, L, 2, M_HEADS), 1.0),
        'ada_w': nrm((L, D, N_MOD * D), 0.5 * D ** -0.5),
        'ada_b': nrm((L, N_MOD * D), 0.02),
        'norm1_g': gain((L, D)),
        'norm2_g': gain((L, D)),
        'w_in': nrm((L, D, IN_DIM), D ** -0.5),
        'b_in': b_in,
        'qn_g': gain((L, A_DIM)),
        'kn_g': gain((L, A_DIM)),
        'lam_q1': nrm((L, A_DIM), 0.1),
        'lam_k1': nrm((L, A_DIM), 0.1),
        'lam_q2': nrm((L, A_DIM), 0.1),
        'lam_k2': nrm((L, A_DIM), 0.1),
        'subln_g': gain((L, A_VDIM)),
        'conv_w': nrm((L, CONV_WIDTH, B_WIDTH), CONV_WIDTH ** -0.5),
        'conv_b': nrm((L, B_WIDTH), 0.02),
        'cln_g': gain((L, B_WIDTH)),
        'cln_b': nrm((L, B_WIDTH), 0.02),
        'mnorm_g': gain((L, M_WIDTH)),
        'w_br_a': nrm((L, A_WIDTH, D), A_WIDTH ** -0.5),
        'w_br_b': nrm((L, B_WIDTH, D), B_WIDTH ** -0.5),
        'w_br_c': nrm((L, M_WIDTH, D), M_WIDTH ** -0.5),
        'w_out': nrm((L, D, D), D ** -0.5),
        'router_w': nrm((L, D, N_EXPERTS), D ** -0.5),
        'router_b': nrm((L, N_EXPERTS), 0.01),
        'w_gu': nrm((L, N_EXPERTS, D, 2 * D_FF), D ** -0.5),
        'b_gu': nrm((L, N_EXPERTS, 2 * D_FF), 0.02),
        'w_dn': nrm((L, N_EXPERTS, D_FF, D), D_FF ** -0.5),
        'b_dn': nrm((L, N_EXPERTS, D), 0.02),
    }


def reference(x_prompt, x_sample, c, c_ctx, cache_k, cache_v, state_C, state_n, state_m,
              ada_w, ada_b, norm1_g, norm2_g, w_in, b_in, qn_g, kn_g,
              lam_q1, lam_k1, lam_q2, lam_k2, subln_g, conv_w, conv_b, cln_g, cln_b, mnorm_g,
              w_br_a, w_br_b, w_br_c, w_out, router_w, router_b, w_gu, b_gu, w_dn, b_dn):
    weights = dict(ada_w=ada_w, ada_b=ada_b, norm1_g=norm1_g, norm2_g=norm2_g, w_in=w_in, b_in=b_in,
                   qn_g=qn_g, kn_g=kn_g, lam_q1=lam_q1, lam_k1=lam_k1, lam_q2=lam_q2, lam_k2=lam_k2,
                   subln_g=subln_g, conv_w=conv_w, conv_b=conv_b, cln_g=cln_g, cln_b=cln_b,
                   mnorm_g=mnorm_g, w_br_a=w_br_a, w_br_b=w_br_b, w_br_c=w_br_c, w_out=w_out,
                   router_w=router_w, router_b=router_b, w_gu=w_gu, b_gu=b_gu, w_dn=w_dn, b_dn=b_dn)
    y_p, y_s = x_prompt, x_sample
    ks, vs, Cs, ns, ms = [], [], [], [], []
    for l in range(DEPTH):
        p = {name: w[l] for name, w in weights.items()}
        lam_init = 0.8 - 0.6 * math.exp(-0.3 * l)
        mod_ctx = (jax.nn.silu(c_ctx[None, :]) @ p['ada_w'] + p['ada_b'])[:, None, :]
        y_p, (kc, vc, Cc, nc, mc) = trunk_layer(y_p, mod_ctx, p, lam_init, None)
        ks.append(kc); vs.append(vc); Cs.append(Cc); ns.append(nc); ms.append(mc)
        mod_lat = (jax.nn.silu(c) @ p['ada_w'] + p['ada_b'])[:, None, :]
        ctx = (cache_k[:, l], cache_v[:, l], state_C[:, l], state_n[:, l], state_m[:, l])
        y_s, _ = trunk_layer(y_s, mod_lat, p, lam_init, ctx)
    new_cache_k = jnp.stack(ks, axis=1)
    new_cache_v = jnp.stack(vs, axis=1)
    new_state_C = jnp.stack(Cs, axis=1)
    new_state_n = jnp.stack(ns, axis=1)
    new_state_m = jnp.stack(ms, axis=1)
    return (y_p, y_s, new_cache_k, new_cache_v, new_state_C, new_state_n, new_state_m)
```

```python
import functools
import math

import jax
import jax.numpy as jnp
import numpy as np
from jax import lax
from jax.experimental import pallas as pl
from jax.experimental.pallas import tpu as pltpu

F32 = jnp.float32
BF16 = jnp.bfloat16

D = 1024
BATCH = 32
SEQ = 256
DEPTH = 2
DEC_BATCH = 4
DEC_SEQ = 1024
PAST_LEN = 512
GRID_W = 64
EPS = 1e-6
A_HEADS = 8
A_DIM = 64
A_VDIM = 128
ROPE_BASE = 10000.0
CONV_WIDTH = 31
CONV_PAD = CONV_WIDTH // 2
M_HEADS = 4
M_DIM = 256
N_EXPERTS = 32
TOP_K = 4
SWIGLU_LIMIT = 7.0
SWIGLU_ALPHA = 1.702
N_MOD = 6

TP = BATCH * SEQ
TS = DEC_BATCH * DEC_SEQ
T = TP + TS
N_MAIN = 12 * D
GATE_OFF = 9 * D
GL_OFF = GATE_OFF + 16

LANE = 128
SUBLANE = 8
VMEM_LIMIT = 56 * 1024 * 1024

TM_IN = 1024
TN_IN = 1024
TM_MERGE = 256
TQ = 256
M_CHUNK = 256
CONV_TILE = 256
CONV_HALO = 16
CONV_ROWS = 32
CONV_COLS = 512
MOE_BM = 256
MOE_NB = T * TOP_K // MOE_BM + N_EXPERTS
MOE_ROWS = MOE_NB * MOE_BM
TOK_TILE = 128


def _sigmoid(x):
    return 1.0 / (1.0 + jnp.exp(-x))


def _log_sigmoid(x):
    return jnp.minimum(x, 0.0) - jnp.log1p(jnp.exp(-jnp.abs(x)))


def _dot(a, b):
    return jnp.dot(a, b, preferred_element_type=F32)


def _dot_nt(a, b):
    return lax.dot_general(a, b, (((1,), (1,)), ((), ())), preferred_element_type=F32)


def _dot_tn(a, b):
    return lax.dot_general(a, b, (((0,), (0,)), ((), ())), preferred_element_type=F32)


def _params(n_axes, vmem=None):
    return pltpu.CompilerParams(dimension_semantics=("arbitrary",) * n_axes,
                                vmem_limit_bytes=vmem)


def _ada_kernel(c_ref, w_ref, b_ref, o_ref):
    c = c_ref[...]
    s = (c * _sigmoid(c)).astype(BF16)
    o_ref[0] = _dot(s, w_ref[0].astype(BF16)) + b_ref[0]


def _ada_call(cond, ada_w, ada_b):
    n = N_MOD * D
    return pl.pallas_call(
        _ada_kernel,
        out_shape=jax.ShapeDtypeStruct((DEPTH, SUBLANE, n), F32),
        grid=(DEPTH, n // D),
        in_specs=[pl.BlockSpec((SUBLANE, D), lambda l, j: (0, 0)),
                  pl.BlockSpec((1, D, D), lambda l, j: (l, 0, j)),
                  pl.BlockSpec((1, 1, D), lambda l, j: (l, 0, j))],
        out_specs=pl.BlockSpec((1, SUBLANE, D), lambda l, j: (l, 0, j)),
        compiler_params=_params(2),
        name="ada_mod",
    )(cond, ada_w, ada_b.reshape(DEPTH, 1, n))


def _mod_row_in(i):
    return jnp.where(i < TP // TM_IN, 0, i - (TP // TM_IN - 1))


def _inproj_kernel(x_ref, mod_ref, g_ref, w_ref, b_ref, wg_ref, bg_ref, o_ref, og_ref, xn_ref):
    @pl.when(pl.program_id(1) == 0)
    def _():
        x = x_ref[...]
        ms = jnp.mean(x * x, axis=-1, keepdims=True)
        y = x * lax.rsqrt(ms + EPS) * g_ref[...]
        xn = (y * (1.0 + mod_ref[0, 1:2, :]) + mod_ref[0, 0:1, :]).astype(BF16)
        xn_ref[...] = xn
        og_ref[...] = _dot(xn, wg_ref[...]) + bg_ref[...]

    o_ref[...] = _dot(xn_ref[...], w_ref[...]) + b_ref[...]


def _inproj_call(x, mod, g, w_main, b_main, w_gate, b_gate):
    return pl.pallas_call(
        _inproj_kernel,
        out_shape=(jax.ShapeDtypeStruct((T, N_MAIN), F32),
                   jax.ShapeDtypeStruct((T, LANE), F32)),
        grid=(T // TM_IN, N_MAIN // TN_IN),
        in_specs=[pl.BlockSpec((TM_IN, D), lambda i, j: (i, 0)),
                  pl.BlockSpec((1, N_MOD, D), lambda i, j: (_mod_row_in(i), 0, 0)),
                  pl.BlockSpec((1, D), lambda i, j: (0, 0)),
                  pl.BlockSpec((D, TN_IN), lambda i, j: (0, j)),
                  pl.BlockSpec((1, TN_IN), lambda i, j: (0, j)),
                  pl.BlockSpec((D, LANE), lambda i, j: (0, 0)),
                  pl.BlockSpec((1, LANE), lambda i, j: (0, 0))],
        out_specs=(pl.BlockSpec((TM_IN, TN_IN), lambda i, j: (i, j)),
                   pl.BlockSpec((TM_IN, LANE), lambda i, j: (i, 0))),
        scratch_shapes=[pltpu.VMEM((TM_IN, D), BF16)],
        compiler_params=_params(2, VMEM_LIMIT),
        name="in_proj",
    )(x, mod, g, w_main, b_main, w_gate, b_gate)


def _qk_norm(x, g2):
    lo = lax.broadcasted_iota(jnp.int32, x.shape, 1) < A_DIM
    x2 = x * x
    s0 = jnp.sum(jnp.where(lo, x2, 0.0), axis=-1, keepdims=True)
    s1 = jnp.sum(jnp.where(lo, 0.0, x2), axis=-1, keepdims=True)
    ms = jnp.where(lo, s0, s1) * (1.0 / A_DIM)
    return x * lax.rsqrt(ms + EPS) * g2


def _rope(x, cos, sin_signed):
    first = (lax.broadcasted_iota(jnp.int32, x.shape, 1) % 32) < 16
    partner = jnp.where(first, pltpu.roll(x, LANE - 16, 1), pltpu.roll(x, 16, 1))
    return x * cos + partner * sin_signed


def _lambda(lamp_ref, lam_init):
    l1 = jnp.sum(lamp_ref[0:1, :] * lamp_ref[1:2, :], axis=-1, keepdims=True)
    l2 = jnp.sum(lamp_ref[2:3, :] * lamp_ref[3:4, :], axis=-1, keepdims=True)
    return jnp.exp(l1) - jnp.exp(l2) + lam_init


def _softmax(s):
    e = jnp.exp(s - jnp.max(s, axis=-1, keepdims=True))
    return e / jnp.sum(e, axis=-1, keepdims=True)


def _diff_attn_head(qh, kh, vh, lam, sg, lam_init):
    lo = lax.broadcasted_iota(jnp.int32, qh.shape, 1) < A_DIM
    qs = qh * (A_DIM ** -0.5)
    q0 = jnp.where(lo, qs, 0.0).astype(BF16)
    q1 = jnp.where(lo, 0.0, qs).astype(BF16)
    a0 = _softmax(_dot_nt(q0, kh))
    a1 = _softmax(_dot_nt(q1, kh))
    w = (a0 - lam * a1).astype(BF16)
    o = _dot(w, vh)
    ms = jnp.mean(o * o, axis=-1, keepdims=True)
    return o * lax.rsqrt(ms + EPS) * sg * (1.0 - lam_init)


def _attn_ctx_kernel(lamp_ref, q_ref, k_ref, v_ref, qg_ref, kg_ref, sg_ref, o_ref, kn_ref, *, lam_init):
    lam = _lambda(lamp_ref, lam_init)
    for h in range(A_HEADS):
        sl = slice(h * LANE, (h + 1) * LANE)
        kn = _qk_norm(k_ref[:, sl], kg_ref[...])
        kn_ref[:, sl] = kn
        qn = _qk_norm(q_ref[:, sl], qg_ref[...])
        o = _diff_attn_head(qn, kn.astype(BF16), v_ref[:, sl].astype(BF16), lam, sg_ref[...], lam_init)
        o_ref[:, sl] = o.astype(BF16)


def _attn_ctx_call(proj, lamp, qg2, kg2, sg, lam_init):
    blk = lambda c: pl.BlockSpec((SEQ, D), lambda b, c=c: (b, c))
    vec = pl.BlockSpec((1, LANE), lambda b: (0, 0))
    return pl.pallas_call(
        functools.partial(_attn_ctx_kernel, lam_init=lam_init),
        out_shape=(jax.ShapeDtypeStruct((TP, D), BF16),
                   jax.ShapeDtypeStruct((TP, D), F32)),
        grid=(BATCH,),
        in_specs=[pl.BlockSpec((4, A_DIM), lambda b: (0, 0)), blk(0), blk(1), blk(2), vec, vec, vec],
        out_specs=(pl.BlockSpec((SEQ, D), lambda b: (b, 0)),
                   pl.BlockSpec((SEQ, D), lambda b: (b, 0))),
        compiler_params=_params(1, VMEM_LIMIT),
        name="attn_ctx",
    )(lamp, proj, proj, proj, qg2, kg2, sg)


def _attn_lat_kernel(lamp_ref, q_ref, k_ref, v_ref, kc_ref, vc_ref, cos_ref, sin_ref, cosq_ref, sinq_ref,
                     qg_ref, kg_ref, sg_ref, o_ref, kall_ref, vall_ref, *, lam_init):
    @pl.when(pl.program_id(1) == 0)
    def _():
        for h in range(A_HEADS):
            sl = slice(h * LANE, (h + 1) * LANE)
            kn = _rope(_qk_norm(k_ref[:, sl], kg_ref[...]), cos_ref[...], sin_ref[...])
            kall_ref[0:DEC_SEQ, sl] = kn.astype(BF16)
        vall_ref[0:DEC_SEQ, :] = v_ref[...].astype(BF16)
        kall_ref[DEC_SEQ:, :] = kc_ref[0, 0].astype(BF16)
        vall_ref[DEC_SEQ:, :] = vc_ref[0, 0].astype(BF16)

    lam = _lambda(lamp_ref, lam_init)
    for h in range(A_HEADS):
        sl = slice(h * LANE, (h + 1) * LANE)
        qn = _rope(_qk_norm(q_ref[:, sl], qg_ref[...]), cosq_ref[...], sinq_ref[...])
        o = _diff_attn_head(qn, kall_ref[:, sl], vall_ref[:, sl], lam, sg_ref[...], lam_init)
        o_ref[:, sl] = o.astype(BF16)


def _attn_lat_call(proj, cache_k, cache_v, layer, rope_cos, rope_sin, lamp, qg2, kg2, sg, lam_init):
    nq = DEC_SEQ // TQ
    r0 = TP // DEC_SEQ
    vec = pl.BlockSpec((1, LANE), lambda b, i: (0, 0))
    ctx = pl.BlockSpec((1, 1, PAST_LEN, D), lambda b, i: (b, layer, 0, 0))
    full = lambda c: pl.BlockSpec((DEC_SEQ, D), lambda b, i, c=c: (r0 + b, c))
    return pl.pallas_call(
        functools.partial(_attn_lat_kernel, lam_init=lam_init),
        out_shape=jax.ShapeDtypeStruct((TS, D), BF16),
        grid=(DEC_BATCH, nq),
        in_specs=[pl.BlockSpec((4, A_DIM), lambda b, i: (0, 0)),
                  pl.BlockSpec((TQ, D), lambda b, i: ((TP // TQ) + b * nq + i, 0)),
                  full(1), full(2), ctx, ctx,
                  pl.BlockSpec((DEC_SEQ, LANE), lambda b, i: (0, 0)),
                  pl.BlockSpec((DEC_SEQ, LANE), lambda b, i: (0, 0)),
                  pl.BlockSpec((TQ, LANE), lambda b, i: (i, 0)),
                  pl.BlockSpec((TQ, LANE), lambda b, i: (i, 0)),
                  vec, vec, vec],
        out_specs=pl.BlockSpec((TQ, D), lambda b, i: (b * nq + i, 0)),
        scratch_shapes=[pltpu.VMEM((DEC_SEQ + PAST_LEN, D), BF16),
                        pltpu.VMEM((DEC_SEQ + PAST_LEN, D), BF16)],
        compiler_params=_params(2, VMEM_LIMIT),
        name="attn_lat",
    )(lamp, proj, proj, proj, cache_k, cache_v, rope_cos, rope_sin, rope_cos, rope_sin, qg2, kg2, sg)


def _rope_tables():
    t = np.arange(DEC_SEQ)
    row, col = t // GRID_W, t % GRID_W
    lane = np.arange(LANE)
    jj = lane % A_DIM
    freq = ROPE_BASE ** (-(jj % 16).astype(np.float64) / 16.0)
    pos = np.where((jj // 32)[None, :] == 0, row[:, None], col[:, None]).astype(np.float64)
    ang = pos.astype(np.float32) * freq.astype(np.float32)[None, :]
    sign = np.where((jj % 32) < 16, -1.0, 1.0).astype(np.float32)
    return jnp.cos(jnp.asarray(ang)), jnp.sin(jnp.asarray(ang)) * jnp.asarray(sign)[None, :]


def _conv_neighbours(i):
    j = (i - TP // CONV_TILE) % (DEC_SEQ // CONV_TILE)
    lat = i >= TP // CONV_TILE
    return lat & (j > 0), lat & (j < DEC_SEQ // CONV_TILE - 1)


def _conv_kernel(a_ref, gate_ref, ap_ref, gp_ref, an_ref, gn_ref, w_ref, b_ref, g_ref, beta_ref, o_ref,
                 hp_ref, acc_ref):
    halo = CONV_HALO
    seq = CONV_TILE
    has_prev, has_next = _conv_neighbours(pl.program_id(0))
    hp_ref[0:halo, :] = jnp.where(has_prev, ap_ref[...] * _sigmoid(gp_ref[...]), 0.0)
    hp_ref[halo + seq:, :] = jnp.where(has_next, an_ref[...] * _sigmoid(gn_ref[...]), 0.0)
    hp_ref[halo:halo + seq, :] = a_ref[...] * _sigmoid(gate_ref[...])

    for t0 in range(0, seq, CONV_ROWS):
        for c0 in range(0, D, CONV_COLS):
            acc = jnp.zeros((CONV_ROWS, CONV_COLS), F32)
            for j in range(CONV_WIDTH):
                r = halo - CONV_PAD + t0 + j
                acc = acc + w_ref[j:j + 1, c0:c0 + CONV_COLS] * hp_ref[r:r + CONV_ROWS, c0:c0 + CONV_COLS]
            acc_ref[t0:t0 + CONV_ROWS, c0:c0 + CONV_COLS] = acc + b_ref[:, c0:c0 + CONV_COLS]

    h = acc_ref[...]
    mu = jnp.mean(h, axis=-1, keepdims=True)
    hc = h - mu
    y = hc * lax.rsqrt(jnp.mean(hc * hc, axis=-1, keepdims=True) + EPS) * g_ref[...] + beta_ref[...]
    o_ref[...] = (y * _sigmoid(y)).astype(BF16)


def _conv_call(proj, conv_w, conv_b, cln_g, cln_b):
    vec = pl.BlockSpec((1, D), lambda i: (0, 0))
    per = CONV_TILE // CONV_HALO
    cur = lambda c: pl.BlockSpec((CONV_TILE, D), lambda i, c=c: (i, c))
    prev = lambda c: pl.BlockSpec((CONV_HALO, D), lambda i, c=c: (jnp.maximum(i * per - 1, 0), c))
    nxt = lambda c: pl.BlockSpec((CONV_HALO, D), lambda i, c=c: (jnp.minimum((i + 1) * per, T // CONV_HALO - 1), c))
    return pl.pallas_call(
        _conv_kernel,
        out_shape=jax.ShapeDtypeStruct((T, D), BF16),
        grid=(T // CONV_TILE,),
        in_specs=[cur(3), cur(4), prev(3), prev(4), nxt(3), nxt(4),
                  pl.BlockSpec((CONV_WIDTH, D), lambda i: (0, 0)),
                  vec, vec, vec],
        out_specs=pl.BlockSpec((CONV_TILE, D), lambda i: (i, 0)),
        scratch_shapes=[pltpu.VMEM((CONV_TILE + 2 * CONV_HALO, D), F32), pltpu.VMEM((CONV_TILE, D), F32)],
        compiler_params=_params(1, VMEM_LIMIT),
        name="conformer_conv",
    )(proj, proj, proj, proj, proj, proj, conv_w, conv_b, cln_g, cln_b)


def _mlstm_kernel(*refs, seq, has_state, emit_state, layer):
    if has_state:
        (m0_ref, q_ref, k_ref, v_ref, og_ref, gc_ref, gr_ref, mg_ref, c0_ref, n0_ref) = refs[:10]
        rest = refs[10:]
    else:
        (q_ref, k_ref, v_ref, og_ref, gc_ref, gr_ref, mg_ref) = refs[:7]
        rest = refs[7:]
    if emit_state:
        o_ref, cout_ref, nm_ref, hacc_ref, c_ref = rest
    else:
        o_ref, hacc_ref, c_ref = rest

    b_id = pl.program_id(0)
    h_id = pl.program_id(1)
    L = M_CHUNK
    nc = seq // L
    t_idx = lax.broadcasted_iota(jnp.int32, (L, L), 0)
    s_idx = lax.broadcasted_iota(jnp.int32, (L, L), 1)

    for d in range(2):
        mask = (s_idx <= t_idx) if d == 0 else (s_idx >= t_idx)
        mask_t = (t_idx <= s_idx) if d == 0 else (t_idx >= s_idx)
        last = L - 1 if d == 0 else 0
        if has_state:
            c_ref[...] = c0_ref[0, 0, d, 0]
            n = n0_ref[0, 0, d, pl.ds(h_id, 1), :]
            m = jnp.full((1, 1), m0_ref[b_id, layer, d, h_id], F32)
        else:
            n = jnp.zeros((1, M_DIM), F32)
            m = jnp.zeros((1, 1), F32)
        chunks = range(nc) if d == 0 else range(nc - 1, -1, -1)
        for ci, c in enumerate(chunks):
            rows = slice(c * L, (c + 1) * L)
            first_chunk = ci == 0
            last_chunk = ci == nc - 1
            q = q_ref[rows, :]
            k = k_ref[rows, :] * (M_DIM ** -0.5)
            qb, kb, vb = q.astype(BF16), k.astype(BF16), v_ref[rows, :].astype(BF16)
            gi_col = gc_ref[0, rows, d:d + 1]
            gi_row = gr_ref[0, d:d + 1, rows]
            lf_col = _log_sigmoid(gc_ref[0, rows, 2 + d:3 + d])
            lf_row = _log_sigmoid(gr_ref[0, 2 + d:3 + d, rows])
            b_col = jnp.sum(jnp.where(mask, lf_row, 0.0), axis=1, keepdims=True)
            b_row = jnp.sum(jnp.where(mask_t, lf_col, 0.0), axis=0, keepdims=True)
            dmat = jnp.where(mask, b_col - b_row + gi_row, -jnp.inf)
            inter = b_col + m
            mt = jnp.maximum(inter, jnp.max(dmat, axis=1, keepdims=True))
            w = jnp.exp(dmat - mt)
            s_inter = jnp.exp(inter - mt)
            sw = _dot_nt(qb, kb) * w
            num = _dot(sw.astype(BF16), vb)
            den = jnp.sum(sw, axis=1, keepdims=True)
            if has_state or not first_chunk:
                num = num + s_inter * _dot(qb, c_ref[...].astype(BF16))
                den = den + s_inter * jnp.sum(q * n, axis=1, keepdims=True)
            hout = num / jnp.maximum(jnp.abs(den), jnp.exp(-mt))
            if d == 0:
                hacc_ref[rows, :] = hout
            else:
                hacc_ref[rows, :] = hacc_ref[rows, :] + hout
            if emit_state or not last_chunk:
                m_new = mt[last:last + 1, :]
                btot = b_col[last:last + 1, :]
                decay = jnp.exp(btot + m - m_new)
                wk_col = jnp.exp(btot - b_col + gi_col - m_new)
                kw = k * wk_col
                upd = _dot_tn(kw.astype(BF16), vb)
                nsum = jnp.sum(kw, axis=0, keepdims=True)
                if has_state or not first_chunk:
                    c_ref[...] = decay * c_ref[...] + upd
                    n = decay * n + nsum
                else:
                    c_ref[...] = upd
                    n = nsum
                m = m_new
        if emit_state:
            cout_ref[0, d, 0] = c_ref[...]
            nm_ref[0, 0, d:d + 1, 0:M_DIM] = n
            nm_ref[0, 0, d:d + 1, M_DIM:] = jnp.broadcast_to(m, (1, LANE))

    hm = hacc_ref[...]
    ms = jnp.mean(hm * hm, axis=-1, keepdims=True)
    y = hm * lax.rsqrt(ms + EPS) * mg_ref[...]
    o_ref[...] = (y * _sigmoid(og_ref[...])).astype(BF16)


def _mlstm_call(proj, gates_col, gates_row, mnorm_g, seq, nb, row0, layer, states=None):
    has_state = states is not None
    emit_state = not has_state
    col = lambda c0: pl.BlockSpec((seq, M_DIM), lambda b, h, c0=c0: (row0 + b, c0 + h))
    in_specs = [col(20), col(24), col(28), col(32),
                pl.BlockSpec((1, seq, 4), lambda b, h: (h, row0 + b, 0)),
                pl.BlockSpec((1, 4, seq), lambda b, h: (h, 0, row0 + b)),
                pl.BlockSpec((1, M_DIM), lambda b, h: (0, h))]
    args = [proj, proj, proj, proj, gates_col, gates_row, mnorm_g]
    if has_state:
        state_c, state_n, state_m = states
        in_specs = [pl.BlockSpec(memory_space=pltpu.SMEM)] + in_specs + [
            pl.BlockSpec((1, 1, 2, 1, M_DIM, M_DIM), lambda b, h: (b, layer, 0, h, 0, 0)),
            pl.BlockSpec((1, 1, 2, M_HEADS, M_DIM), lambda b, h: (b, layer, 0, 0, 0))]
        args = [state_m] + args + [state_c, state_n]
    out_shape = [jax.ShapeDtypeStruct((nb * seq, D), BF16)]
    out_specs = [pl.BlockSpec((seq, M_DIM), lambda b, h: (b, h))]
    if emit_state:
        out_shape += [jax.ShapeDtypeStruct((nb, 2, M_HEADS, M_DIM, M_DIM), F32),
                      jax.ShapeDtypeStruct((nb, M_HEADS, 2, M_DIM + LANE), F32)]
        out_specs += [pl.BlockSpec((1, 2, 1, M_DIM, M_DIM), lambda b, h: (b, 0, h, 0, 0)),
                      pl.BlockSpec((1, 1, 2, M_DIM + LANE), lambda b, h: (b, h, 0, 0))]
    return pl.pallas_call(
        functools.partial(_mlstm_kernel, seq=seq, has_state=has_state, emit_state=emit_state, layer=layer),
        out_shape=tuple(out_shape),
        grid=(nb, M_HEADS),
        in_specs=in_specs,
        out_specs=tuple(out_specs),
        scratch_shapes=[pltpu.VMEM((seq, M_DIM), F32), pltpu.VMEM((M_DIM, M_DIM), F32)],
        compiler_params=_params(2, VMEM_LIMIT),
        name="mlstm",
    )(*args)


def _mod_row_merge(i):
    per = DEC_SEQ // TM_MERGE
    return jnp.where(i < TP // TM_MERGE, 0, 1 + (i - TP // TM_MERGE) // per)


def _split3(x):
    hi = x.astype(BF16)
    lo = (x - hi.astype(F32)).astype(BF16)
    return hi, lo


def _merge_kernel(oap, oas, ob_ref, ocp, ocs, gl_ref, x_ref, mod_ref, wa_ref, wb_ref, wc_ref, wo_ref,
                  g2_ref, rwh_ref, rwl_ref, rb_ref, x1_ref, xn_ref, route_ref, cnt_ref, carry_ref):
    i = pl.program_id(0)
    is_ctx = i < TP // TM_MERGE

    @pl.when(i == 0)
    def _():
        carry_ref[...] = jnp.zeros_like(carry_ref)

    def branch(p_ref, s_ref, w_ref, c):
        o = jnp.where(is_ctx, p_ref[...], s_ref[...])
        return _sigmoid(gl_ref[:, c * D:(c + 1) * D]) * _dot(o, w_ref[...])

    merged = (branch(oap, oas, wa_ref, 0)
              + _sigmoid(gl_ref[:, D:2 * D]) * _dot(ob_ref[...], wb_ref[...])
              + branch(ocp, ocs, wc_ref, 2))
    out = _dot(merged.astype(BF16), wo_ref[...])
    x1 = x_ref[...] + mod_ref[0, 2:3, :] * out
    x1_ref[...] = x1
    ms = jnp.mean(x1 * x1, axis=-1, keepdims=True)
    xn = x1 * lax.rsqrt(ms + EPS) * g2_ref[...] * (1.0 + mod_ref[0, 4:5, :]) + mod_ref[0, 3:4, :]
    xn_ref[...] = xn

    xh, xl = _split3(xn)
    logits = _dot(xh, rwh_ref[...]) + _dot(xh, rwl_ref[...]) + _dot(xl, rwh_ref[...]) + rb_ref[...]
    lane = lax.broadcasted_iota(jnp.int32, logits.shape, 1)
    lanef = lane.astype(F32)
    cur = jnp.where(lane < N_EXPERTS, logits, -jnp.inf)
    vals, idxs = [], []
    for _ in range(TOP_K):
        mx = jnp.max(cur, axis=-1, keepdims=True)
        idx = jnp.min(jnp.where(cur == mx, lanef, float(LANE)), axis=-1, keepdims=True)
        vals.append(mx)
        idxs.append(idx)
        cur = jnp.where(lanef == idx, -jnp.inf, cur)
    exps = [jnp.exp(v - vals[0]) for v in vals]
    den = exps[0] + exps[1] + exps[2] + exps[3]

    onehot = jnp.zeros(logits.shape, F32)
    for idx in idxs:
        onehot = onehot + jnp.where(lanef == idx, 1.0, 0.0)
    r_idx = lax.broadcasted_iota(jnp.int32, (TM_MERGE, TM_MERGE), 0)
    c_idx = lax.broadcasted_iota(jnp.int32, (TM_MERGE, TM_MERGE), 1)
    tril = jnp.where(c_idx <= r_idx, 1.0, 0.0).astype(BF16)
    incl = _dot(tril, onehot.astype(BF16))
    tot = carry_ref[...] + incl
    route = jnp.zeros(logits.shape, F32)
    for k in range(TOP_K):
        rank = jnp.sum(jnp.where(lanef == idxs[k], tot, 0.0), axis=-1, keepdims=True) - 1.0
        route = jnp.where(lane == k, idxs[k], route)
        route = jnp.where(lane == TOP_K + k, exps[k] / den, route)
        route = jnp.where(lane == 2 * TOP_K + k, rank, route)
    route_ref[...] = route
    carry_ref[...] = tot[TM_MERGE - 1:TM_MERGE, :]
    cnt_ref[...] = jnp.broadcast_to(tot[TM_MERGE - 1:TM_MERGE, :], cnt_ref.shape)


def _merge_call(oa, ob, oc, proj, x, mod, wa, wb, wc, wo, g2, rwh, rwl, rb):
    npt = TP // TM_MERGE
    pblk = pl.BlockSpec((TM_MERGE, D), lambda i: (jnp.minimum(i, npt - 1), 0))
    sblk = pl.BlockSpec((TM_MERGE, D), lambda i: (jnp.maximum(i - npt, 0), 0))
    wblk = pl.BlockSpec((D, D), lambda i: (0, 0))
    rblk = pl.BlockSpec((D, LANE), lambda i: (0, 0))
    return pl.pallas_call(
        _merge_kernel,
        out_shape=(jax.ShapeDtypeStruct((T, D), F32),
                   jax.ShapeDtypeStruct((T, D), F32),
                   jax.ShapeDtypeStruct((T, LANE), F32),
                   jax.ShapeDtypeStruct((SUBLANE, LANE), F32)),
        grid=(T // TM_MERGE,),
        in_specs=[pblk, sblk, pl.BlockSpec((TM_MERGE, D), lambda i: (i, 0)), pblk, sblk,
                  pl.BlockSpec((TM_MERGE, 3 * D), lambda i: (i, 3)),
                  pl.BlockSpec((TM_MERGE, D), lambda i: (i, 0)),
                  pl.BlockSpec((1, N_MOD, D), lambda i: (_mod_row_merge(i), 0, 0)),
                  wblk, wblk, wblk, wblk,
                  pl.BlockSpec((1, D), lambda i: (0, 0)),
                  rblk, rblk,
                  pl.BlockSpec((1, LANE), lambda i: (0, 0))],
        out_specs=(pl.BlockSpec((TM_MERGE, D), lambda i: (i, 0)),
                   pl.BlockSpec((TM_MERGE, D), lambda i: (i, 0)),
                   pl.BlockSpec((TM_MERGE, LANE), lambda i: (i, 0)),
                   pl.BlockSpec((SUBLANE, LANE), lambda i: (0, 0))),
        scratch_shapes=[pltpu.VMEM((1, LANE), F32)],
        compiler_params=_params(1, VMEM_LIMIT),
        name="merge_router",
    )(oa[0], oa[1], ob, oc[0], oc[1], proj, x, mod, wa, wb, wc, wo, g2, rwh, rwl, rb)


def _row_copy(src_ref, src_row, dst_ref, dst_row, sem):
    return pltpu.make_async_copy(src_ref.at[pl.ds(src_row, 1), :], dst_ref.at[pl.ds(dst_row, 1), :], sem)


def _scatter_kernel(pos_ref, x_ref, zeros_ref, xs_ref, sem):
    del zeros_ref

    def start(r, carry):
        for k in range(TOP_K):
            _row_copy(x_ref, r, xs_ref, pos_ref[0, 0, r * TOP_K + k], sem).start()
        return carry

    lax.fori_loop(0, TOK_TILE, start, 0)

    def wait(r, carry):
        for k in range(TOP_K):
            _row_copy(x_ref, r, xs_ref, pos_ref[0, 0, r * TOP_K + k], sem).wait()
        return carry

    lax.fori_loop(0, TOK_TILE, wait, 0)


def _scatter_call(pos3, xn):
    zeros = jnp.zeros((MOE_ROWS, D), F32)
    return pl.pallas_call(
        _scatter_kernel,
        out_shape=jax.ShapeDtypeStruct((MOE_ROWS, D), F32),
        grid=(T // TOK_TILE,),
        in_specs=[pl.BlockSpec((1, 1, TOK_TILE * TOP_K), lambda i: (i, 0, 0), memory_space=pltpu.SMEM),
                  pl.BlockSpec((TOK_TILE, D), lambda i: (i, 0)),
                  pl.BlockSpec(memory_space=pl.ANY)],
        out_specs=pl.BlockSpec(memory_space=pl.ANY),
        scratch_shapes=[pltpu.SemaphoreType.DMA],
        input_output_aliases={2: 0},
        compiler_params=_params(1),
        name="moe_scatter",
    )(pos3, xn, zeros)


def _gmm_kernel(be_ref, nused_ref, x_ref, wgu_ref, bgu_ref, wdn_ref, bdn_ref, y_ref, wgu_bf, wdn_bf):
    i = pl.program_id(0)
    prev = be_ref[jnp.maximum(i - 1, 0)]

    @pl.when((i == 0) | (be_ref[i] != prev))
    def _():
        wgu_bf[...] = wgu_ref[0].astype(BF16)
        wdn_bf[...] = wdn_ref[0].astype(BF16)

    @pl.when(i < nused_ref[0])
    def _():
        gu = _dot(x_ref[...].astype(BF16), wgu_bf[...]) + bgu_ref[0]
        g = jnp.minimum(gu[:, :D], SWIGLU_LIMIT)
        u = jnp.clip(gu[:, D:], -SWIGLU_LIMIT, SWIGLU_LIMIT)
        hdn = (u + 1.0) * (g * _sigmoid(SWIGLU_ALPHA * g))
        y_ref[...] = _dot(hdn.astype(BF16), wdn_bf[...]) + bdn_ref[0]

    @pl.when(i >= nused_ref[0])
    def _():
        y_ref[...] = jnp.zeros_like(y_ref)


def _gmm_call(block_e, nused, xs, w_gu, b_gu, w_dn, b_dn):
    grid_spec = pltpu.PrefetchScalarGridSpec(
        num_scalar_prefetch=2,
        grid=(MOE_NB,),
        in_specs=[pl.BlockSpec((MOE_BM, D), lambda i, be, nu: (i, 0)),
                  pl.BlockSpec((1, D, 2 * D), lambda i, be, nu: (be[i], 0, 0)),
                  pl.BlockSpec((1, 1, 2 * D), lambda i, be, nu: (be[i], 0, 0)),
                  pl.BlockSpec((1, D, D), lambda i, be, nu: (be[i], 0, 0)),
                  pl.BlockSpec((1, 1, D), lambda i, be, nu: (be[i], 0, 0))],
        out_specs=pl.BlockSpec((MOE_BM, D), lambda i, be, nu: (i, 0)),
        scratch_shapes=[pltpu.VMEM((D, 2 * D), BF16), pltpu.VMEM((D, D), BF16)],
    )
    return pl.pallas_call(
        _gmm_kernel,
        out_shape=jax.ShapeDtypeStruct((MOE_ROWS, D), F32),
        grid_spec=grid_spec,
        compiler_params=_params(1, VMEM_LIMIT),
        name="moe_experts",
    )(block_e, nused, xs, w_gu, b_gu.reshape(N_EXPERTS, 1, 2 * D), w_dn, b_dn.reshape(N_EXPERTS, 1, D))


def _mod_row_tok(i):
    per = DEC_SEQ // TOK_TILE
    return jnp.where(i < TP // TOK_TILE, 0, 1 + (i - TP // TOK_TILE) // per)


def _combine_kernel(pos_ref, ys_ref, x1_ref, route_ref, mod_ref, o_ref, buf_ref, sem):
    def start(r, carry):
        for k in range(TOP_K):
            _row_copy(ys_ref, pos_ref[0, 0, r * TOP_K + k], buf_ref.at[k], r, sem).start()
        return carry

    lax.fori_loop(0, TOK_TILE, start, 0)

    def wait(r, carry):
        for k in range(TOP_K):
            _row_copy(ys_ref, pos_ref[0, 0, r * TOP_K + k], buf_ref.at[k], r, sem).wait()
        return carry

    lax.fori_loop(0, TOK_TILE, wait, 0)

    acc = jnp.zeros((TOK_TILE, D), F32)
    for k in range(TOP_K):
        acc = acc + route_ref[:, TOP_K + k:TOP_K + k + 1] * buf_ref[k]
    o_ref[...] = x1_ref[...] + mod_ref[0, 5:6, :] * acc


def _combine_call(pos3, ys, x1, route, mod):
    return pl.pallas_call(
        _combine_kernel,
        out_shape=jax.ShapeDtypeStruct((T, D), F32),
        grid=(T // TOK_TILE,),
        in_specs=[pl.BlockSpec((1, 1, TOK_TILE * TOP_K), lambda i: (i, 0, 0), memory_space=pltpu.SMEM),
                  pl.BlockSpec(memory_space=pl.ANY),
                  pl.BlockSpec((TOK_TILE, D), lambda i: (i, 0)),
                  pl.BlockSpec((TOK_TILE, LANE), lambda i: (i, 0)),
                  pl.BlockSpec((1, N_MOD, D), lambda i: (_mod_row_tok(i), 0, 0))],
        out_specs=pl.BlockSpec((TOK_TILE, D), lambda i: (i, 0)),
        scratch_shapes=[pltpu.VMEM((TOP_K, TOK_TILE, D), F32), pltpu.SemaphoreType.DMA],
        compiler_params=_params(1, VMEM_LIMIT),
        name="moe_combine",
    )(pos3, ys, x1, route, mod)


def _routing_tables(route, counts):
    top_e = route[:, 0:TOP_K].astype(jnp.int32)
    rank = route[:, 2 * TOP_K:3 * TOP_K].astype(jnp.int32)
    cnt = counts[0, :N_EXPERTS].astype(jnp.int32)
    padded = (cnt + MOE_BM - 1) // MOE_BM * MOE_BM
    pad_end = jnp.cumsum(padded)
    pad_start = pad_end - padded
    pos = pad_start[top_e] + rank
    block_e = jnp.minimum(jnp.searchsorted(pad_end, jnp.arange(MOE_NB, dtype=jnp.int32) * MOE_BM, side='right'),
                          N_EXPERTS - 1).astype(jnp.int32)
    nused = (pad_end[-1:] // MOE_BM).astype(jnp.int32)
    return pos.reshape(T // TOK_TILE, 1, TOK_TILE * TOP_K), block_e, nused


def kernel(x_prompt, x_sample, c, c_ctx, cache_k, cache_v, state_C, state_n, state_m, ada_w, ada_b, norm1_g, norm2_g, w_in, b_in, qn_g, kn_g, lam_q1, lam_k1, lam_q2, lam_k2, subln_g, conv_w, conv_b, cln_g, cln_b, mnorm_g, w_br_a, w_br_b, w_br_c, w_out, router_w, router_b, w_gu, b_gu, w_dn, b_dn):
    x = jnp.concatenate([x_prompt.reshape(TP, D), x_sample.reshape(TS, D)], axis=0)
    cond = jnp.concatenate([c_ctx[None, :], c, jnp.zeros((SUBLANE - 1 - DEC_BATCH, D), F32)], axis=0)
    mod_all = _ada_call(cond, ada_w, ada_b).reshape(DEPTH, SUBLANE, N_MOD, D)
    rope_cos, rope_sin = _rope_tables()
    cache_k4 = cache_k.reshape(DEC_BATCH, DEPTH, PAST_LEN, D)
    cache_v4 = cache_v.reshape(DEC_BATCH, DEPTH, PAST_LEN, D)

    ks, vs, cs, ns, ms = [], [], [], [], []
    for l in range(DEPTH):
        lam_init = 0.8 - 0.6 * math.exp(-0.3 * l)
        mod = mod_all[l]
        w = w_in[l]
        w_main = jnp.concatenate([w[:, :GATE_OFF], w[:, GL_OFF:]], axis=1).astype(BF16)
        b_main = jnp.concatenate([b_in[l, :GATE_OFF], b_in[l, GL_OFF:]])[None, :]
        w_gate = jnp.pad(w[:, GATE_OFF:GL_OFF], ((0, 0), (0, LANE - 16))).astype(BF16)
        b_gate = jnp.pad(b_in[l, GATE_OFF:GL_OFF], (0, LANE - 16))[None, :]
        proj, gates = _inproj_call(x, mod, norm1_g[l][None, :], w_main, b_main, w_gate, b_gate)

        g4 = gates[:, :16].reshape(T, 2, 2, M_HEADS)
        gates_col = jnp.transpose(g4, (3, 0, 1, 2)).reshape(M_HEADS, T, 4)
        gates_row = jnp.transpose(g4, (3, 1, 2, 0)).reshape(M_HEADS, 4, T)

        lamp = jnp.stack([lam_q1[l], lam_k1[l], lam_q2[l], lam_k2[l]])
        qg2 = jnp.tile(qn_g[l], 2)[None, :]
        kg2 = jnp.tile(kn_g[l], 2)[None, :]
        sg = subln_g[l][None, :]
        oa_p, k_new = _attn_ctx_call(proj, lamp, qg2, kg2, sg, lam_init)
        oa_s = _attn_lat_call(proj, cache_k4, cache_v4, l, rope_cos, rope_sin, lamp, qg2, kg2, sg, lam_init)

        cw, cb, cg, cbeta = conv_w[l], conv_b[l][None, :], cln_g[l][None, :], cln_b[l][None, :]
        ob = _conv_call(proj, cw, cb, cg, cbeta)

        mg = mnorm_g[l][None, :]
        oc_p, c_new, nm_new = _mlstm_call(proj, gates_col, gates_row, mg, SEQ, BATCH, 0, l)
        (oc_s,) = _mlstm_call(proj, gates_col, gates_row, mg, DEC_SEQ, DEC_BATCH, TP // DEC_SEQ, l,
                              states=(state_C, state_n, state_m))

        rw = jnp.pad(router_w[l], ((0, 0), (0, LANE - N_EXPERTS)))
        rwh = rw.astype(BF16)
        rwl = (rw - rwh.astype(F32)).astype(BF16)
        rb = jnp.pad(router_b[l], (0, LANE - N_EXPERTS))[None, :]
        x1, xn2, route, counts = _merge_call(
            (oa_p, oa_s), ob, (oc_p, oc_s), proj, x, mod,
            w_br_a[l].astype(BF16), w_br_b[l].astype(BF16), w_br_c[l].astype(BF16), w_out[l].astype(BF16),
            norm2_g[l][None, :], rwh, rwl, rb)

        pos3, block_e, nused = _routing_tables(route, counts)
        xs = _scatter_call(pos3, xn2)
        ys = _gmm_call(block_e, nused, xs, w_gu[l], b_gu[l], w_dn[l], b_dn[l])
        x = _combine_call(pos3, ys, x1, route, mod)

        ks.append(k_new.reshape(BATCH, SEQ, A_HEADS, 2 * A_DIM))
        vs.append(proj[:TP, 2 * D:3 * D].reshape(BATCH, SEQ, A_HEADS, A_VDIM))
        cs.append(c_new)
        ns.append(jnp.transpose(nm_new[..., :M_DIM], (0, 2, 1, 3)))
        ms.append(jnp.transpose(nm_new[..., M_DIM], (0, 2, 1)))

    y_p = x[:TP].reshape(BATCH, SEQ, D)
    y_s = x[TP:].reshape(DEC_BATCH, DEC_SEQ, D)
    return (y_p, y_s, jnp.stack(ks, axis=1), jnp.stack(vs, axis=1), jnp.stack(cs, axis=1),
            jnp.stack(ns, axis=1), jnp.stack(ms, axis=1))
```

```python
import functools
import math

import jax
import jax.numpy as jnp
import numpy as np
from jax import lax
from jax.experimental import pallas as pl
from jax.experimental.pallas import tpu as pltpu

F32 = jnp.float32
BF16 = jnp.bfloat16

D = 1024
BATCH = 32
SEQ = 256
DEPTH = 2
DEC_BATCH = 4
DEC_SEQ = 1024
PAST_LEN = 512
GRID_W = 64
EPS = 1e-6
A_HEADS = 8
A_DIM = 64
A_VDIM = 128
ROPE_BASE = 10000.0
CONV_WIDTH = 31
CONV_PAD = CONV_WIDTH // 2
M_HEADS = 4
M_DIM = 256
N_EXPERTS = 32
TOP_K = 4
SWIGLU_LIMIT = 7.0
SWIGLU_ALPHA = 1.702
N_MOD = 6

TP = BATCH * SEQ
TS = DEC_BATCH * DEC_SEQ
T = TP + TS
N_MAIN = 12 * D
GATE_OFF = 9 * D
GL_OFF = GATE_OFF + 16

LANE = 128
SUBLANE = 8
VMEM_LIMIT = 56 * 1024 * 1024

TM_IN = 1024
TN_IN = 1024
TM_MERGE = 256
TQ = 256
M_CHUNK = 256
CONV_TILE = 256
CONV_HALO = 16
CONV_ROWS = 32
CONV_COLS = 512
MOE_BM = 256
MOE_NB = T * TOP_K // MOE_BM + N_EXPERTS
MOE_ROWS = MOE_NB * MOE_BM
TOK_TILE = 128


def _sigmoid(x):
    return 1.0 / (1.0 + jnp.exp(-x))


def _log_sigmoid(x):
    return jnp.minimum(x, 0.0) - jnp.log1p(jnp.exp(-jnp.abs(x)))


def _dot(a, b):
    return jnp.dot(a, b, preferred_element_type=F32)


def _dot_nt(a, b):
    return lax.dot_general(a, b, (((1,), (1,)), ((), ())), preferred_element_type=F32)


def _dot_tn(a, b):
    return lax.dot_general(a, b, (((0,), (0,)), ((), ())), preferred_element_type=F32)


def _params(n_axes, vmem=None):
    return pltpu.CompilerParams(dimension_semantics=("arbitrary",) * n_axes,
                                vmem_limit_bytes=vmem)


def _ada_kernel(c_ref, w_ref, b_ref, o_ref):
    c = c_ref[...]
    s = (c * _sigmoid(c)).astype(BF16)
    o_ref[0] = _dot(s, w_ref[0].astype(BF16)) + b_ref[0]


def _ada_call(cond, ada_w, ada_b):
    n = N_MOD * D
    return pl.pallas_call(
        _ada_kernel,
        out_shape=jax.ShapeDtypeStruct((DEPTH, SUBLANE, n), F32),
        grid=(DEPTH, n // D),
        in_specs=[pl.BlockSpec((SUBLANE, D), lambda l, j: (0, 0)),
                  pl.BlockSpec((1, D, D), lambda l, j: (l, 0, j)),
                  pl.BlockSpec((1, 1, D), lambda l, j: (l, 0, j))],
        out_specs=pl.BlockSpec((1, SUBLANE, D), lambda l, j: (l, 0, j)),
        compiler_params=_params(2),
        name="ada_mod",
    )(cond, ada_w, ada_b.reshape(DEPTH, 1, n))


def _mod_row_in(i):
    return jnp.where(i < TP // TM_IN, 0, i - (TP // TM_IN - 1))


def _inproj_kernel(x_ref, mod_ref, g_ref, w_ref, b_ref, wg_ref, bg_ref, o_ref, og_ref, xn_ref):
    @pl.when(pl.program_id(1) == 0)
    def _():
        x = x_ref[...]
        ms = jnp.mean(x * x, axis=-1, keepdims=True)
        y = x * lax.rsqrt(ms + EPS) * g_ref[...]
        xn = (y * (1.0 + mod_ref[0, 1:2, :]) + mod_ref[0, 0:1, :]).astype(BF16)
        xn_ref[...] = xn
        og_ref[...] = _dot(xn, wg_ref[...]) + bg_ref[...]

    o_ref[...] = _dot(xn_ref[...], w_ref[...]) + b_ref[...]


def _inproj_call(x, mod, g, w_main, b_main, w_gate, b_gate):
    return pl.pallas_call(
        _inproj_kernel,
        out_shape=(jax.ShapeDtypeStruct((T, N_MAIN), F32),
                   jax.ShapeDtypeStruct((T, LANE), F32)),
        grid=(T // TM_IN, N_MAIN // TN_IN),
        in_specs=[pl.BlockSpec((TM_IN, D), lambda i, j: (i, 0)),
                  pl.BlockSpec((1, N_MOD, D), lambda i, j: (_mod_row_in(i), 0, 0)),
                  pl.BlockSpec((1, D), lambda i, j: (0, 0)),
                  pl.BlockSpec((D, TN_IN), lambda i, j: (0, j)),
                  pl.BlockSpec((1, TN_IN), lambda i, j: (0, j)),
                  pl.BlockSpec((D, LANE), lambda i, j: (0, 0)),
                  pl.BlockSpec((1, LANE), lambda i, j: (0, 0))],
        out_specs=(pl.BlockSpec((TM_IN, TN_IN), lambda i, j: (i, j)),
                   pl.BlockSpec((TM_IN, LANE), lambda i, j: (i, 0))),
        scratch_shapes=[pltpu.VMEM((TM_IN, D), BF16)],
        compiler_params=_params(2, VMEM_LIMIT),
        name="in_proj",
    )(x, mod, g, w_main, b_main, w_gate, b_gate)


def _qk_norm(x, g2):
    lo = lax.broadcasted_iota(jnp.int32, x.shape, 1) < A_DIM
    x2 = x * x
    s0 = jnp.sum(jnp.where(lo, x2, 0.0), axis=-1, keepdims=True)
    s1 = jnp.sum(jnp.where(lo, 0.0, x2), axis=-1, keepdims=True)
    ms = jnp.where(lo, s0, s1) * (1.0 / A_DIM)
    return x * lax.rsqrt(ms + EPS) * g2


def _rope(x, cos, sin_signed):
    first = (lax.broadcasted_iota(jnp.int32, x.shape, 1) % 32) < 16
    partner = jnp.where(first, pltpu.roll(x, LANE - 16, 1), pltpu.roll(x, 16, 1))
    return x * cos + partner * sin_signed


def _lambda(lamp_ref, lam_init):
    l1 = jnp.sum(lamp_ref[0:1, :] * lamp_ref[1:2, :], axis=-1, keepdims=True)
    l2 = jnp.sum(lamp_ref[2:3, :] * lamp_ref[3:4, :], axis=-1, keepdims=True)
    return jnp.exp(l1) - jnp.exp(l2) + lam_init


def _softmax(s):
    e = jnp.exp(s - jnp.max(s, axis=-1, keepdims=True))
    return e / jnp.sum(e, axis=-1, keepdims=True)


def _diff_attn_head(qh, kh, vh, lam, sg, lam_init):
    lo = lax.broadcasted_iota(jnp.int32, qh.shape, 1) < A_DIM
    qs = qh * (A_DIM ** -0.5)
    q0 = jnp.where(lo, qs, 0.0).astype(BF16)
    q1 = jnp.where(lo, 0.0, qs).astype(BF16)
    a0 = _softmax(_dot_nt(q0, kh))
    a1 = _softmax(_dot_nt(q1, kh))
    w = (a0 - lam * a1).astype(BF16)
    o = _dot(w, vh)
    ms = jnp.mean(o * o, axis=-1, keepdims=True)
    return o * lax.rsqrt(ms + EPS) * sg * (1.0 - lam_init)


def _attn_ctx_kernel(*refs, lam_init, n_carried):
    lamp_ref, q_ref, k_ref, v_ref, qg_ref, kg_ref, sg_ref = refs[:7]
    o_ref, kn_ref, vc_ref = refs[7 + n_carried:]
    lam = _lambda(lamp_ref, lam_init)
    n_slots = kn_ref.shape[1]
    for s in range(n_slots):
        vc_ref[0, s] = v_ref[...]
    for h in range(A_HEADS):
        sl = slice(h * LANE, (h + 1) * LANE)
        kn = _qk_norm(k_ref[:, sl], kg_ref[...])
        for s in range(n_slots):
            kn_ref[0, s, :, sl] = kn
        qn = _qk_norm(q_ref[:, sl], qg_ref[...])
        o = _diff_attn_head(qn, kn.astype(BF16), v_ref[:, sl].astype(BF16), lam, sg_ref[...], lam_init)
        o_ref[:, sl] = o.astype(BF16)


def _attn_ctx_call(proj, lamp, qg2, kg2, sg, lam_init, layer, carried):
    blk = lambda c: pl.BlockSpec((SEQ, D), lambda b, c=c: (b, c))
    vec = pl.BlockSpec((1, LANE), lambda b: (0, 0))
    if carried:
        cache = pl.BlockSpec((1, 1, SEQ, D), lambda b: (b, layer, 0, 0))
    else:
        cache = pl.BlockSpec((1, DEPTH, SEQ, D), lambda b: (b, 0, 0, 0))
    cache_shape = jax.ShapeDtypeStruct((BATCH, DEPTH, SEQ, D), F32)
    n_in = 7
    return pl.pallas_call(
        functools.partial(_attn_ctx_kernel, lam_init=lam_init, n_carried=len(carried)),
        out_shape=(jax.ShapeDtypeStruct((TP, D), BF16), cache_shape, cache_shape),
        grid=(BATCH,),
        in_specs=[pl.BlockSpec((4, A_DIM), lambda b: (0, 0)), blk(0), blk(1), blk(2), vec, vec, vec]
                 + [pl.BlockSpec(memory_space=pl.ANY)] * len(carried),
        out_specs=(pl.BlockSpec((SEQ, D), lambda b: (b, 0)), cache, cache),
        input_output_aliases={n_in + j: 1 + j for j in range(len(carried))},
        compiler_params=_params(1, VMEM_LIMIT),
        name="attn_ctx",
    )(lamp, proj, proj, proj, qg2, kg2, sg, *carried)


def _attn_lat_kernel(lamp_ref, q_ref, k_ref, v_ref, kc_ref, vc_ref, cos_ref, sin_ref, cosq_ref, sinq_ref,
                     qg_ref, kg_ref, sg_ref, o_ref, kall_ref, vall_ref, *, lam_init):
    @pl.when(pl.program_id(1) == 0)
    def _():
        for h in range(A_HEADS):
            sl = slice(h * LANE, (h + 1) * LANE)
            kn = _rope(_qk_norm(k_ref[:, sl], kg_ref[...]), cos_ref[...], sin_ref[...])
            kall_ref[0:DEC_SEQ, sl] = kn.astype(BF16)
        vall_ref[0:DEC_SEQ, :] = v_ref[...].astype(BF16)
        kall_ref[DEC_SEQ:, :] = kc_ref[0, 0].astype(BF16)
        vall_ref[DEC_SEQ:, :] = vc_ref[0, 0].astype(BF16)

    lam = _lambda(lamp_ref, lam_init)
    for h in range(A_HEADS):
        sl = slice(h * LANE, (h + 1) * LANE)
        qn = _rope(_qk_norm(q_ref[:, sl], qg_ref[...]), cosq_ref[...], sinq_ref[...])
        o = _diff_attn_head(qn, kall_ref[:, sl], vall_ref[:, sl], lam, sg_ref[...], lam_init)
        o_ref[:, sl] = o.astype(BF16)


def _attn_lat_call(proj, cache_k, cache_v, layer, rope_cos, rope_sin, lamp, qg2, kg2, sg, lam_init):
    nq = DEC_SEQ // TQ
    r0 = TP // DEC_SEQ
    vec = pl.BlockSpec((1, LANE), lambda b, i: (0, 0))
    ctx = pl.BlockSpec((1, 1, PAST_LEN, D), lambda b, i: (b, layer, 0, 0))
    full = lambda c: pl.BlockSpec((DEC_SEQ, D), lambda b, i, c=c: (r0 + b, c))
    return pl.pallas_call(
        functools.partial(_attn_lat_kernel, lam_init=lam_init),
        out_shape=jax.ShapeDtypeStruct((TS, D), BF16),
        grid=(DEC_BATCH, nq),
        in_specs=[pl.BlockSpec((4, A_DIM), lambda b, i: (0, 0)),
                  pl.BlockSpec((TQ, D), lambda b, i: ((TP // TQ) + b * nq + i, 0)),
                  full(1), full(2), ctx, ctx,
                  pl.BlockSpec((DEC_SEQ, LANE), lambda b, i: (0, 0)),
                  pl.BlockSpec((DEC_SEQ, LANE), lambda b, i: (0, 0)),
                  pl.BlockSpec((TQ, LANE), lambda b, i: (i, 0)),
                  pl.BlockSpec((TQ, LANE), lambda b, i: (i, 0)),
                  vec, vec, vec],
        out_specs=pl.BlockSpec((TQ, D), lambda b, i: (b * nq + i, 0)),
        scratch_shapes=[pltpu.VMEM((DEC_SEQ + PAST_LEN, D), BF16),
                        pltpu.VMEM((DEC_SEQ + PAST_LEN, D), BF16)],
        compiler_params=_params(2, VMEM_LIMIT),
        name="attn_lat",
    )(lamp, proj, proj, proj, cache_k, cache_v, rope_cos, rope_sin, rope_cos, rope_sin, qg2, kg2, sg)


def _rope_tables():
    t = np.arange(DEC_SEQ)
    row, col = t // GRID_W, t % GRID_W
    lane = np.arange(LANE)
    jj = lane % A_DIM
    freq = ROPE_BASE ** (-(jj % 16).astype(np.float64) / 16.0)
    pos = np.where((jj // 32)[None, :] == 0, row[:, None], col[:, None]).astype(np.float64)
    ang = pos.astype(np.float32) * freq.astype(np.float32)[None, :]
    sign = np.where((jj % 32) < 16, -1.0, 1.0).astype(np.float32)
    return jnp.cos(jnp.asarray(ang)), jnp.sin(jnp.asarray(ang)) * jnp.asarray(sign)[None, :]


def _conv_neighbours(i):
    j = (i - TP // CONV_TILE) % (DEC_SEQ // CONV_TILE)
    lat = i >= TP // CONV_TILE
    return lat & (j > 0), lat & (j < DEC_SEQ // CONV_TILE - 1)


def _conv_kernel(a_ref, gate_ref, ap_ref, gp_ref, an_ref, gn_ref, w_ref, b_ref, g_ref, beta_ref, o_ref,
                 hp_ref, hs_ref, acc_ref):
    halo = CONV_HALO
    seq = CONV_TILE
    has_prev, has_next = _conv_neighbours(pl.program_id(0))
    hp_ref[0:halo, :] = jnp.where(has_prev, ap_ref[...] * _sigmoid(gp_ref[...]), 0.0)
    hp_ref[halo + seq:, :] = jnp.where(has_next, an_ref[...] * _sigmoid(gn_ref[...]), 0.0)
    hp_ref[halo:halo + seq, :] = a_ref[...] * _sigmoid(gate_ref[...])

    srows = hs_ref.shape[1]
    for s in range(SUBLANE):
        hs_ref[s] = hp_ref[s:s + srows, :]

    for t0 in range(0, seq, CONV_ROWS):
        for c0 in range(0, D, CONV_COLS):
            acc = jnp.zeros((CONV_ROWS, CONV_COLS), F32)
            for j in range(CONV_WIDTH):
                r = halo - CONV_PAD + t0 + j
                s = r % SUBLANE
                acc = acc + (w_ref[j:j + 1, c0:c0 + CONV_COLS]
                             * hs_ref[s, r - s:r - s + CONV_ROWS, c0:c0 + CONV_COLS])
            acc_ref[t0:t0 + CONV_ROWS, c0:c0 + CONV_COLS] = acc + b_ref[:, c0:c0 + CONV_COLS]

    h = acc_ref[...]
    mu = jnp.mean(h, axis=-1, keepdims=True)
    hc = h - mu
    y = hc * lax.rsqrt(jnp.mean(hc * hc, axis=-1, keepdims=True) + EPS) * g_ref[...] + beta_ref[...]
    o_ref[...] = (y * _sigmoid(y)).astype(BF16)


def _conv_call(proj, conv_w, conv_b, cln_g, cln_b):
    vec = pl.BlockSpec((1, D), lambda i: (0, 0))
    per = CONV_TILE // CONV_HALO
    cur = lambda c: pl.BlockSpec((CONV_TILE, D), lambda i, c=c: (i, c))
    prev = lambda c: pl.BlockSpec((CONV_HALO, D), lambda i, c=c: (jnp.maximum(i * per - 1, 0), c))
    nxt = lambda c: pl.BlockSpec((CONV_HALO, D), lambda i, c=c: (jnp.minimum((i + 1) * per, T // CONV_HALO - 1), c))
    return pl.pallas_call(
        _conv_kernel,
        out_shape=jax.ShapeDtypeStruct((T, D), BF16),
        grid=(T // CONV_TILE,),
        in_specs=[cur(3), cur(4), prev(3), prev(4), nxt(3), nxt(4),
                  pl.BlockSpec((CONV_WIDTH, D), lambda i: (0, 0)),
                  vec, vec, vec],
        out_specs=pl.BlockSpec((CONV_TILE, D), lambda i: (i, 0)),
        scratch_shapes=[pltpu.VMEM((CONV_TILE + 2 * CONV_HALO, D), F32),
                        pltpu.VMEM((SUBLANE, CONV_TILE + 2 * CONV_HALO - SUBLANE, D), F32),
                        pltpu.VMEM((CONV_TILE, D), F32)],
        compiler_params=_params(1, VMEM_LIMIT),
        name="conformer_conv",
    )(proj, proj, proj, proj, proj, proj, conv_w, conv_b, cln_g, cln_b)


def _mlstm_kernel(*refs, seq, has_state, emit_state, layer, n_carried):
    if has_state:
        (m0_ref, q_ref, k_ref, v_ref, og_ref, gc_ref, gr_ref, mg_ref, c0_ref, n0_ref) = refs[:10]
        rest = refs[10:]
    else:
        (q_ref, k_ref, v_ref, og_ref, gc_ref, gr_ref, mg_ref) = refs[:7]
        rest = refs[7 + n_carried:]
    if emit_state:
        o_ref, cout_ref, nm_ref, hacc_ref, c_ref = rest
    else:
        o_ref, hacc_ref, c_ref = rest

    b_id = pl.program_id(0)
    h_id = pl.program_id(1)
    L = M_CHUNK
    nc = seq // L
    t_idx = lax.broadcasted_iota(jnp.int32, (L, L), 0)
    s_idx = lax.broadcasted_iota(jnp.int32, (L, L), 1)

    for d in range(2):
        mask = (s_idx <= t_idx) if d == 0 else (s_idx >= t_idx)
        mask_t = (t_idx <= s_idx) if d == 0 else (t_idx >= s_idx)
        last = L - 1 if d == 0 else 0
        if has_state:
            c_ref[...] = c0_ref[0, 0, d, 0]
            n = n0_ref[0, 0, d, pl.ds(h_id, 1), :]
            m = jnp.full((1, 1), m0_ref[b_id, layer, d, h_id], F32)
        else:
            n = jnp.zeros((1, M_DIM), F32)
            m = jnp.zeros((1, 1), F32)
        chunks = range(nc) if d == 0 else range(nc - 1, -1, -1)
        for ci, c in enumerate(chunks):
            rows = slice(c * L, (c + 1) * L)
            first_chunk = ci == 0
            last_chunk = ci == nc - 1
            q = q_ref[rows, :]
            k = k_ref[rows, :] * (M_DIM ** -0.5)
            qb, kb, vb = q.astype(BF16), k.astype(BF16), v_ref[rows, :].astype(BF16)
            gi_col = gc_ref[0, rows, d:d + 1]
            gi_row = gr_ref[0, d:d + 1, rows]
            lf_col = _log_sigmoid(gc_ref[0, rows, 2 + d:3 + d])
            lf_row = _log_sigmoid(gr_ref[0, 2 + d:3 + d, rows])
            b_col = jnp.sum(jnp.where(mask, lf_row, 0.0), axis=1, keepdims=True)
            b_row = jnp.sum(jnp.where(mask_t, lf_col, 0.0), axis=0, keepdims=True)
            dmat = jnp.where(mask, b_col - b_row + gi_row, -jnp.inf)
            inter = b_col + m
            mt = jnp.maximum(inter, jnp.max(dmat, axis=1, keepdims=True))
            w = jnp.exp(dmat - mt)
            s_inter = jnp.exp(inter - mt)
            sw = _dot_nt(qb, kb) * w
            num = _dot(sw.astype(BF16), vb)
            den = jnp.sum(sw, axis=1, keepdims=True)
            if has_state or not first_chunk:
                num = num + s_inter * _dot(qb, c_ref[...].astype(BF16))
                den = den + s_inter * jnp.sum(q * n, axis=1, keepdims=True)
            hout = num / jnp.maximum(jnp.abs(den), jnp.exp(-mt))
            if d == 0:
                hacc_ref[rows, :] = hout
            else:
                hacc_ref[rows, :] = hacc_ref[rows, :] + hout
            if emit_state or not last_chunk:
                m_new = mt[last:last + 1, :]
                btot = b_col[last:last + 1, :]
                decay = jnp.exp(btot + m - m_new)
                wk_col = jnp.exp(btot - b_col + gi_col - m_new)
                kw = k * wk_col
                upd = _dot_tn(kw.astype(BF16), vb)
                nsum = jnp.sum(kw, axis=0, keepdims=True)
                if has_state or not first_chunk:
                    c_ref[...] = decay * c_ref[...] + upd
                    n = decay * n + nsum
                else:
                    c_ref[...] = upd
                    n = nsum
                m = m_new
        if emit_state:
            for s in range(cout_ref.shape[1]):
                cout_ref[0, s, d, 0] = c_ref[...]
            nm_ref[0, 0, d:d + 1, 0:M_DIM] = n
            nm_ref[0, 0, d:d + 1, M_DIM:] = jnp.broadcast_to(m, (1, LANE))

    hm = hacc_ref[...]
    ms = jnp.mean(hm * hm, axis=-1, keepdims=True)
    y = hm * lax.rsqrt(ms + EPS) * mg_ref[...]
    o_ref[...] = (y * _sigmoid(og_ref[...])).astype(BF16)


def _mlstm_call(proj, gates_col, gates_row, mnorm_g, seq, nb, row0, layer, states=None, carried=()):
    has_state = states is not None
    emit_state = not has_state
    aliases = {}
    col = lambda c0: pl.BlockSpec((seq, M_DIM), lambda b, h, c0=c0: (row0 + b, c0 + h))
    in_specs = [col(20), col(24), col(28), col(32),
                pl.BlockSpec((1, seq, 4), lambda b, h: (h, row0 + b, 0)),
                pl.BlockSpec((1, 4, seq), lambda b, h: (h, 0, row0 + b)),
                pl.BlockSpec((1, M_DIM), lambda b, h: (0, h))]
    args = [proj, proj, proj, proj, gates_col, gates_row, mnorm_g]
    if has_state:
        state_c, state_n, state_m = states
        in_specs = [pl.BlockSpec(memory_space=pltpu.SMEM)] + in_specs + [
            pl.BlockSpec((1, 1, 2, 1, M_DIM, M_DIM), lambda b, h: (b, layer, 0, h, 0, 0)),
            pl.BlockSpec((1, 1, 2, M_HEADS, M_DIM), lambda b, h: (b, layer, 0, 0, 0))]
        args = [state_m] + args + [state_c, state_n]
    out_shape = [jax.ShapeDtypeStruct((nb * seq, D), BF16)]
    out_specs = [pl.BlockSpec((seq, M_DIM), lambda b, h: (b, h))]
    if emit_state:
        out_shape += [jax.ShapeDtypeStruct((nb, DEPTH, 2, M_HEADS, M_DIM, M_DIM), F32),
                      jax.ShapeDtypeStruct((nb, M_HEADS, 2, M_DIM + LANE), F32)]
        if carried:
            c_spec = pl.BlockSpec((1, 1, 2, 1, M_DIM, M_DIM), lambda b, h: (b, layer, 0, h, 0, 0))
        else:
            c_spec = pl.BlockSpec((1, DEPTH, 2, 1, M_DIM, M_DIM), lambda b, h: (b, 0, 0, h, 0, 0))
        out_specs += [c_spec, pl.BlockSpec((1, 1, 2, M_DIM + LANE), lambda b, h: (b, h, 0, 0))]
        aliases = {len(args) + j: 1 + j for j in range(len(carried))}
        in_specs = in_specs + [pl.BlockSpec(memory_space=pl.ANY)] * len(carried)
        args = args + list(carried)
    return pl.pallas_call(
        functools.partial(_mlstm_kernel, seq=seq, has_state=has_state, emit_state=emit_state, layer=layer,
                          n_carried=len(carried)),
        out_shape=tuple(out_shape),
        grid=(nb, M_HEADS),
        in_specs=in_specs,
        out_specs=tuple(out_specs),
        input_output_aliases=aliases,
        scratch_shapes=[pltpu.VMEM((seq, M_DIM), F32), pltpu.VMEM((M_DIM, M_DIM), F32)],
        compiler_params=_params(2, VMEM_LIMIT),
        name="mlstm",
    )(*args)


def _mod_row_merge(i):
    per = DEC_SEQ // TM_MERGE
    return jnp.where(i < TP // TM_MERGE, 0, 1 + (i - TP // TM_MERGE) // per)


def _split3(x):
    hi = x.astype(BF16)
    lo = (x - hi.astype(F32)).astype(BF16)
    return hi, lo


def _merge_kernel(oap, oas, ob_ref, ocp, ocs, gl_ref, x_ref, mod_ref, wa_ref, wb_ref, wc_ref, wo_ref,
                  g2_ref, rwh_ref, rwl_ref, rb_ref, x1_ref, xn_ref, route_ref, cnt_ref, carry_ref):
    i = pl.program_id(0)
    is_ctx = i < TP // TM_MERGE

    @pl.when(i == 0)
    def _():
        carry_ref[...] = jnp.zeros_like(carry_ref)

    def branch(p_ref, s_ref, w_ref, c):
        o = jnp.where(is_ctx, p_ref[...], s_ref[...])
        return _sigmoid(gl_ref[:, c * D:(c + 1) * D]) * _dot(o, w_ref[...])

    merged = (branch(oap, oas, wa_ref, 0)
              + _sigmoid(gl_ref[:, D:2 * D]) * _dot(ob_ref[...], wb_ref[...])
              + branch(ocp, ocs, wc_ref, 2))
    out = _dot(merged.astype(BF16), wo_ref[...])
    x1 = x_ref[...] + mod_ref[0, 2:3, :] * out
    x1_ref[...] = x1
    ms = jnp.mean(x1 * x1, axis=-1, keepdims=True)
    xn = x1 * lax.rsqrt(ms + EPS) * g2_ref[...] * (1.0 + mod_ref[0, 4:5, :]) + mod_ref[0, 3:4, :]
    xn_ref[...] = xn

    xh, xl = _split3(xn)
    logits = _dot(xh, rwh_ref[...]) + _dot(xh, rwl_ref[...]) + _dot(xl, rwh_ref[...]) + rb_ref[...]
    lane = lax.broadcasted_iota(jnp.int32, logits.shape, 1)
    lanef = lane.astype(F32)
    cur = jnp.where(lane < N_EXPERTS, logits, -jnp.inf)
    vals, idxs = [], []
    for _ in range(TOP_K):
        mx = jnp.max(cur, axis=-1, keepdims=True)
        idx = jnp.min(jnp.where(cur == mx, lanef, float(LANE)), axis=-1, keepdims=True)
        vals.append(mx)
        idxs.append(idx)
        cur = jnp.where(lanef == idx, -jnp.inf, cur)
    exps = [jnp.exp(v - vals[0]) for v in vals]
    den = exps[0] + exps[1] + exps[2] + exps[3]

    onehot = jnp.zeros(logits.shape, F32)
    for idx in idxs:
        onehot = onehot + jnp.where(lanef == idx, 1.0, 0.0)
    r_idx = lax.broadcasted_iota(jnp.int32, (TM_MERGE, TM_MERGE), 0)
    c_idx = lax.broadcasted_iota(jnp.int32, (TM_MERGE, TM_MERGE), 1)
    tril = jnp.where(c_idx <= r_idx, 1.0, 0.0).astype(BF16)
    incl = _dot(tril, onehot.astype(BF16))
    tot = carry_ref[...] + incl
    route = jnp.zeros(logits.shape, F32)
    for k in range(TOP_K):
        rank = jnp.sum(jnp.where(lanef == idxs[k], tot, 0.0), axis=-1, keepdims=True) - 1.0
        route = jnp.where(lane == k, idxs[k], route)
        route = jnp.where(lane == TOP_K + k, exps[k] / den, route)
        route = jnp.where(lane == 2 * TOP_K + k, rank, route)
    route_ref[...] = route
    carry_ref[...] = tot[TM_MERGE - 1:TM_MERGE, :]
    cnt_ref[...] = jnp.broadcast_to(tot[TM_MERGE - 1:TM_MERGE, :], cnt_ref.shape)


def _merge_call(oa, ob, oc, proj, x, mod, wa, wb, wc, wo, g2, rwh, rwl, rb):
    npt = TP // TM_MERGE
    pblk = pl.BlockSpec((TM_MERGE, D), lambda i: (jnp.minimum(i, npt - 1), 0))
    sblk = pl.BlockSpec((TM_MERGE, D), lambda i: (jnp.maximum(i - npt, 0), 0))
    wblk = pl.BlockSpec((D, D), lambda i: (0, 0))
    rblk = pl.BlockSpec((D, LANE), lambda i: (0, 0))
    return pl.pallas_call(
        _merge_kernel,
        out_shape=(jax.ShapeDtypeStruct((T, D), F32),
                   jax.ShapeDtypeStruct((T, D), F32),
                   jax.ShapeDtypeStruct((T, LANE), F32),
                   jax.ShapeDtypeStruct((SUBLANE, LANE), F32)),
        grid=(T // TM_MERGE,),
        in_specs=[pblk, sblk, pl.BlockSpec((TM_MERGE, D), lambda i: (i, 0)), pblk, sblk,
                  pl.BlockSpec((TM_MERGE, 3 * D), lambda i: (i, 3)),
                  pl.BlockSpec((TM_MERGE, D), lambda i: (i, 0)),
                  pl.BlockSpec((1, N_MOD, D), lambda i: (_mod_row_merge(i), 0, 0)),
                  wblk, wblk, wblk, wblk,
                  pl.BlockSpec((1, D), lambda i: (0, 0)),
                  rblk, rblk,
                  pl.BlockSpec((1, LANE), lambda i: (0, 0))],
        out_specs=(pl.BlockSpec((TM_MERGE, D), lambda i: (i, 0)),
                   pl.BlockSpec((TM_MERGE, D), lambda i: (i, 0)),
                   pl.BlockSpec((TM_MERGE, LANE), lambda i: (i, 0)),
                   pl.BlockSpec((SUBLANE, LANE), lambda i: (0, 0))),
        scratch_shapes=[pltpu.VMEM((1, LANE), F32)],
        compiler_params=_params(1, VMEM_LIMIT),
        name="merge_router",
    )(oa[0], oa[1], ob, oc[0], oc[1], proj, x, mod, wa, wb, wc, wo, g2, rwh, rwl, rb)


def _row_copy(src_ref, src_row, dst_ref, dst_row, sem):
    return pltpu.make_async_copy(src_ref.at[pl.ds(src_row, 1), :], dst_ref.at[pl.ds(dst_row, 1), :], sem)


def _scatter_kernel(pos_ref, x_ref, zeros_ref, xs_ref, sem):
    del zeros_ref

    def start(r, carry):
        for k in range(TOP_K):
            _row_copy(x_ref, r, xs_ref, pos_ref[0, 0, r * TOP_K + k], sem).start()
        return carry

    lax.fori_loop(0, TOK_TILE, start, 0)

    def wait(r, carry):
        for k in range(TOP_K):
            _row_copy(x_ref, r, xs_ref, pos_ref[0, 0, r * TOP_K + k], sem).wait()
        return carry

    lax.fori_loop(0, TOK_TILE, wait, 0)


def _scatter_call(pos3, xn):
    zeros = jnp.zeros((MOE_ROWS, D), F32)
    return pl.pallas_call(
        _scatter_kernel,
        out_shape=jax.ShapeDtypeStruct((MOE_ROWS, D), F32),
        grid=(T // TOK_TILE,),
        in_specs=[pl.BlockSpec((1, 1, TOK_TILE * TOP_K), lambda i: (i, 0, 0), memory_space=pltpu.SMEM),
                  pl.BlockSpec((TOK_TILE, D), lambda i: (i, 0)),
                  pl.BlockSpec(memory_space=pl.ANY)],
        out_specs=pl.BlockSpec(memory_space=pl.ANY),
        scratch_shapes=[pltpu.SemaphoreType.DMA],
        input_output_aliases={2: 0},
        compiler_params=_params(1),
        name="moe_scatter",
    )(pos3, xn, zeros)


def _gmm_kernel(be_ref, nused_ref, x_ref, wgu_ref, bgu_ref, wdn_ref, bdn_ref, y_ref, wgu_bf, wdn_bf):
    i = pl.program_id(0)
    prev = be_ref[jnp.maximum(i - 1, 0)]

    @pl.when((i == 0) | (be_ref[i] != prev))
    def _():
        wgu_bf[...] = wgu_ref[0, 0].astype(BF16)
        wdn_bf[...] = wdn_ref[0, 0].astype(BF16)

    @pl.when(i < nused_ref[0])
    def _():
        gu = _dot(x_ref[...].astype(BF16), wgu_bf[...]) + bgu_ref[0, 0]
        g = jnp.minimum(gu[:, :D], SWIGLU_LIMIT)
        u = jnp.clip(gu[:, D:], -SWIGLU_LIMIT, SWIGLU_LIMIT)
        hdn = (u + 1.0) * (g * _sigmoid(SWIGLU_ALPHA * g))
        y_ref[...] = _dot(hdn.astype(BF16), wdn_bf[...]) + bdn_ref[0, 0]

    @pl.when(i >= nused_ref[0])
    def _():
        y_ref[...] = jnp.zeros_like(y_ref)


def _gmm_call(block_e, nused, xs, w_gu, b_gu, w_dn, b_dn, layer):
    grid_spec = pltpu.PrefetchScalarGridSpec(
        num_scalar_prefetch=2,
        grid=(MOE_NB,),
        in_specs=[pl.BlockSpec((MOE_BM, D), lambda i, be, nu: (i, 0)),
                  pl.BlockSpec((1, 1, D, 2 * D), lambda i, be, nu: (layer, be[i], 0, 0)),
                  pl.BlockSpec((1, 1, 1, 2 * D), lambda i, be, nu: (layer, be[i], 0, 0)),
                  pl.BlockSpec((1, 1, D, D), lambda i, be, nu: (layer, be[i], 0, 0)),
                  pl.BlockSpec((1, 1, 1, D), lambda i, be, nu: (layer, be[i], 0, 0))],
        out_specs=pl.BlockSpec((MOE_BM, D), lambda i, be, nu: (i, 0)),
        scratch_shapes=[pltpu.VMEM((D, 2 * D), BF16), pltpu.VMEM((D, D), BF16)],
    )
    return pl.pallas_call(
        _gmm_kernel,
        out_shape=jax.ShapeDtypeStruct((MOE_ROWS, D), F32),
        grid_spec=grid_spec,
        compiler_params=_params(1, VMEM_LIMIT),
        name="moe_experts",
    )(block_e, nused, xs, w_gu, b_gu.reshape(DEPTH, N_EXPERTS, 1, 2 * D), w_dn,
      b_dn.reshape(DEPTH, N_EXPERTS, 1, D))


def _mod_row_tok(i):
    per = DEC_SEQ // TOK_TILE
    return jnp.where(i < TP // TOK_TILE, 0, 1 + (i - TP // TOK_TILE) // per)


def _combine_kernel(pos_ref, ys_ref, x1_ref, route_ref, mod_ref, o_ref, buf_ref, sem):
    def start(r, carry):
        for k in range(TOP_K):
            _row_copy(ys_ref, pos_ref[0, 0, r * TOP_K + k], buf_ref.at[k], r, sem).start()
        return carry

    lax.fori_loop(0, TOK_TILE, start, 0)

    def wait(r, carry):
        for k in range(TOP_K):
            _row_copy(ys_ref, pos_ref[0, 0, r * TOP_K + k], buf_ref.at[k], r, sem).wait()
        return carry

    lax.fori_loop(0, TOK_TILE, wait, 0)

    acc = jnp.zeros((TOK_TILE, D), F32)
    for k in range(TOP_K):
        acc = acc + route_ref[:, TOP_K + k:TOP_K + k + 1] * buf_ref[k]
    o_ref[...] = x1_ref[...] + mod_ref[0, 5:6, :] * acc


def _combine_call(pos3, ys, x1, route, mod):
    return pl.pallas_call(
        _combine_kernel,
        out_shape=jax.ShapeDtypeStruct((T, D), F32),
        grid=(T // TOK_TILE,),
        in_specs=[pl.BlockSpec((1, 1, TOK_TILE * TOP_K), lambda i: (i, 0, 0), memory_space=pltpu.SMEM),
                  pl.BlockSpec(memory_space=pl.ANY),
                  pl.BlockSpec((TOK_TILE, D), lambda i: (i, 0)),
                  pl.BlockSpec((TOK_TILE, LANE), lambda i: (i, 0)),
                  pl.BlockSpec((1, N_MOD, D), lambda i: (_mod_row_tok(i), 0, 0))],
        out_specs=pl.BlockSpec((TOK_TILE, D), lambda i: (i, 0)),
        scratch_shapes=[pltpu.VMEM((TOP_K, TOK_TILE, D), F32), pltpu.SemaphoreType.DMA],
        compiler_params=_params(1, VMEM_LIMIT),
        name="moe_combine",
    )(pos3, ys, x1, route, mod)


def _routing_tables(route, counts):
    top_e = route[:, 0:TOP_K].astype(jnp.int32)
    rank = route[:, 2 * TOP_K:3 * TOP_K].astype(jnp.int32)
    cnt = counts[0, :N_EXPERTS].astype(jnp.int32)
    padded = (cnt + MOE_BM - 1) // MOE_BM * MOE_BM
    pad_end = jnp.cumsum(padded)
    pad_start = pad_end - padded
    pos = pad_start[top_e] + rank
    starts = jnp.arange(MOE_NB, dtype=jnp.int32) * MOE_BM
    block_e = jnp.minimum(jnp.sum((pad_end[None, :] <= starts[:, None]).astype(jnp.int32), axis=1),
                          N_EXPERTS - 1)
    nused = (pad_end[-1:] // MOE_BM).astype(jnp.int32)
    return pos.reshape(T // TOK_TILE, 1, TOK_TILE * TOP_K), block_e, nused


def kernel(x_prompt, x_sample, c, c_ctx, cache_k, cache_v, state_C, state_n, state_m, ada_w, ada_b, norm1_g, norm2_g, w_in, b_in, qn_g, kn_g, lam_q1, lam_k1, lam_q2, lam_k2, subln_g, conv_w, conv_b, cln_g, cln_b, mnorm_g, w_br_a, w_br_b, w_br_c, w_out, router_w, router_b, w_gu, b_gu, w_dn, b_dn):
    x = jnp.concatenate([x_prompt.reshape(TP, D), x_sample.reshape(TS, D)], axis=0)
    cond = jnp.concatenate([c_ctx[None, :], c, jnp.zeros((SUBLANE - 1 - DEC_BATCH, D), F32)], axis=0)
    mod_all = _ada_call(cond, ada_w, ada_b).reshape(DEPTH, SUBLANE, N_MOD, D)
    rope_cos, rope_sin = _rope_tables()
    cache_k4 = cache_k.reshape(DEC_BATCH, DEPTH, PAST_LEN, D)
    cache_v4 = cache_v.reshape(DEC_BATCH, DEPTH, PAST_LEN, D)

    kv_cache, c_state, ns, ms = (), (), [], []
    for l in range(DEPTH):
        lam_init = 0.8 - 0.6 * math.exp(-0.3 * l)
        mod = mod_all[l]
        w = w_in[l]
        w_main = jnp.concatenate([w[:, :GATE_OFF], w[:, GL_OFF:]], axis=1).astype(BF16)
        b_main = jnp.concatenate([b_in[l, :GATE_OFF], b_in[l, GL_OFF:]])[None, :]
        w_gate = jnp.pad(w[:, GATE_OFF:GL_OFF], ((0, 0), (0, LANE - 16))).astype(BF16)
        b_gate = jnp.pad(b_in[l, GATE_OFF:GL_OFF], (0, LANE - 16))[None, :]
        proj, gates = _inproj_call(x, mod, norm1_g[l][None, :], w_main, b_main, w_gate, b_gate)

        g4 = gates[:, :16].reshape(T, 2, 2, M_HEADS)
        gates_col = jnp.transpose(g4, (3, 0, 1, 2)).reshape(M_HEADS, T, 4)
        gates_row = jnp.transpose(g4, (3, 1, 2, 0)).reshape(M_HEADS, 4, T)

        lamp = jnp.stack([lam_q1[l], lam_k1[l], lam_q2[l], lam_k2[l]])
        qg2 = jnp.tile(qn_g[l], 2)[None, :]
        kg2 = jnp.tile(kn_g[l], 2)[None, :]
        sg = subln_g[l][None, :]
        oa_p, *kv_cache = _attn_ctx_call(proj, lamp, qg2, kg2, sg, lam_init, l, tuple(kv_cache))
        oa_s = _attn_lat_call(proj, cache_k4, cache_v4, l, rope_cos, rope_sin, lamp, qg2, kg2, sg, lam_init)

        cw, cb, cg, cbeta = conv_w[l], conv_b[l][None, :], cln_g[l][None, :], cln_b[l][None, :]
        ob = _conv_call(proj, cw, cb, cg, cbeta)

        mg = mnorm_g[l][None, :]
        oc_p, c_new, nm_new = _mlstm_call(proj, gates_col, gates_row, mg, SEQ, BATCH, 0, l, carried=c_state)
        c_state = (c_new,)
        (oc_s,) = _mlstm_call(proj, gates_col, gates_row, mg, DEC_SEQ, DEC_BATCH, TP // DEC_SEQ, l,
                              states=(state_C, state_n, state_m))

        rw = jnp.pad(router_w[l], ((0, 0), (0, LANE - N_EXPERTS)))
        rwh = rw.astype(BF16)
        rwl = (rw - rwh.astype(F32)).astype(BF16)
        rb = jnp.pad(router_b[l], (0, LANE - N_EXPERTS))[None, :]
        x1, xn2, route, counts = _merge_call(
            (oa_p, oa_s), ob, (oc_p, oc_s), proj, x, mod,
            w_br_a[l].astype(BF16), w_br_b[l].astype(BF16), w_br_c[l].astype(BF16), w_out[l].astype(BF16),
            norm2_g[l][None, :], rwh, rwl, rb)

        pos3, block_e, nused = _routing_tables(route, counts)
        xs = _scatter_call(pos3, xn2)
        ys = _gmm_call(block_e, nused, xs, w_gu, b_gu, w_dn, b_dn, l)
        x = _combine_call(pos3, ys, x1, route, mod)

        ns.append(jnp.transpose(nm_new[..., :M_DIM], (0, 2, 1, 3)))
        ms.append(jnp.transpose(nm_new[..., M_DIM], (0, 2, 1)))

    y_p = x[:TP].reshape(BATCH, SEQ, D)
    y_s = x[TP:].reshape(DEC_BATCH, DEC_SEQ, D)
    new_k = kv_cache[0].reshape(BATCH, DEPTH, SEQ, A_HEADS, 2 * A_DIM)
    new_v = kv_cache[1].reshape(BATCH, DEPTH, SEQ, A_HEADS, A_VDIM)
    return (y_p, y_s, new_k, new_v, c_state[0], jnp.stack(ns, axis=1), jnp.stack(ms, axis=1))
```

```python
import functools
import math

import jax
import jax.numpy as jnp
import numpy as np
from jax import lax
from jax.experimental import pallas as pl
from jax.experimental.pallas import tpu as pltpu

F32 = jnp.float32
BF16 = jnp.bfloat16

D = 1024
BATCH = 32
SEQ = 256
DEPTH = 2
DEC_BATCH = 4
DEC_SEQ = 1024
PAST_LEN = 512
GRID_W = 64
EPS = 1e-6
A_HEADS = 8
A_DIM = 64
A_VDIM = 128
ROPE_BASE = 10000.0
CONV_WIDTH = 31
CONV_PAD = CONV_WIDTH // 2
M_HEADS = 4
M_DIM = 256
N_EXPERTS = 32
TOP_K = 4
SWIGLU_LIMIT = 7.0
SWIGLU_ALPHA = 1.702
N_MOD = 6

TP = BATCH * SEQ
TS = DEC_BATCH * DEC_SEQ
T = TP + TS
N_MAIN = 12 * D
GATE_OFF = 9 * D
GL_OFF = GATE_OFF + 16

LANE = 128
SUBLANE = 8
VMEM_LIMIT = 56 * 1024 * 1024

TM_IN = 1024
TN_IN = 1024
TM_MERGE = 256
TQ = 256
M_CHUNK = 256
CONV_TILE = 256
CONV_HALO = 16
CONV_ROWS = 32
CONV_COLS = 512
MOE_BM = 256
MOE_NB = T * TOP_K // MOE_BM + N_EXPERTS
MOE_ROWS = MOE_NB * MOE_BM
MOE_DUMP = T * TOP_K
TOK_TILE = 256


def _sigmoid(x):
    return 1.0 / (1.0 + jnp.exp(-x))


def _log_sigmoid(x):
    return jnp.minimum(x, 0.0) - jnp.log1p(jnp.exp(-jnp.abs(x)))


def _dot(a, b):
    return jnp.dot(a, b, preferred_element_type=F32)


def _dot_nt(a, b):
    return lax.dot_general(a, b, (((1,), (1,)), ((), ())), preferred_element_type=F32)


def _dot_tn(a, b):
    return lax.dot_general(a, b, (((0,), (0,)), ((), ())), preferred_element_type=F32)


def _params(n_axes, vmem=None):
    return pltpu.CompilerParams(dimension_semantics=("arbitrary",) * n_axes,
                                vmem_limit_bytes=vmem)


def _ada_kernel(c_ref, w_ref, b_ref, o_ref):
    c = c_ref[...]
    s = (c * _sigmoid(c)).astype(BF16)
    o_ref[0] = _dot(s, w_ref[0].astype(BF16)) + b_ref[0]


def _ada_call(cond, ada_w, ada_b):
    n = N_MOD * D
    return pl.pallas_call(
        _ada_kernel,
        out_shape=jax.ShapeDtypeStruct((DEPTH, SUBLANE, n), F32),
        grid=(DEPTH, n // D),
        in_specs=[pl.BlockSpec((SUBLANE, D), lambda l, j: (0, 0)),
                  pl.BlockSpec((1, D, D), lambda l, j: (l, 0, j)),
                  pl.BlockSpec((1, 1, D), lambda l, j: (l, 0, j))],
        out_specs=pl.BlockSpec((1, SUBLANE, D), lambda l, j: (l, 0, j)),
        compiler_params=_params(2),
        name="ada_mod",
    )(cond, ada_w, ada_b.reshape(DEPTH, 1, n))


def _mod_row_in(i):
    return jnp.where(i < TP // TM_IN, 0, i - (TP // TM_IN - 1))


def _inproj_kernel(x_ref, mod_ref, g_ref, w_ref, b_ref, wg_ref, bg_ref, o_ref, og_ref, xn_ref):
    @pl.when(pl.program_id(1) == 0)
    def _():
        x = x_ref[...]
        ms = jnp.mean(x * x, axis=-1, keepdims=True)
        y = x * lax.rsqrt(ms + EPS) * g_ref[...]
        xn = (y * (1.0 + mod_ref[0, 1:2, :]) + mod_ref[0, 0:1, :]).astype(BF16)
        xn_ref[...] = xn
        og_ref[...] = _dot(xn, wg_ref[...]) + bg_ref[...]

    o_ref[...] = _dot(xn_ref[...], w_ref[...]) + b_ref[...]


def _inproj_call(x, mod, g, w_main, b_main, w_gate, b_gate):
    return pl.pallas_call(
        _inproj_kernel,
        out_shape=(jax.ShapeDtypeStruct((T, N_MAIN), F32),
                   jax.ShapeDtypeStruct((T, LANE), F32)),
        grid=(T // TM_IN, N_MAIN // TN_IN),
        in_specs=[pl.BlockSpec((TM_IN, D), lambda i, j: (i, 0)),
                  pl.BlockSpec((1, N_MOD, D), lambda i, j: (_mod_row_in(i), 0, 0)),
                  pl.BlockSpec((1, D), lambda i, j: (0, 0)),
                  pl.BlockSpec((D, TN_IN), lambda i, j: (0, j)),
                  pl.BlockSpec((1, TN_IN), lambda i, j: (0, j)),
                  pl.BlockSpec((D, LANE), lambda i, j: (0, 0)),
                  pl.BlockSpec((1, LANE), lambda i, j: (0, 0))],
        out_specs=(pl.BlockSpec((TM_IN, TN_IN), lambda i, j: (i, j)),
                   pl.BlockSpec((TM_IN, LANE), lambda i, j: (i, 0))),
        scratch_shapes=[pltpu.VMEM((TM_IN, D), BF16)],
        compiler_params=_params(2, VMEM_LIMIT),
        name="in_proj",
    )(x, mod, g, w_main, b_main, w_gate, b_gate)


def _qk_norm(x, g2):
    lo = lax.broadcasted_iota(jnp.int32, x.shape, 1) < A_DIM
    x2 = x * x
    s0 = jnp.sum(jnp.where(lo, x2, 0.0), axis=-1, keepdims=True)
    s1 = jnp.sum(jnp.where(lo, 0.0, x2), axis=-1, keepdims=True)
    ms = jnp.where(lo, s0, s1) * (1.0 / A_DIM)
    return x * lax.rsqrt(ms + EPS) * g2


def _rope(x, cos, sin_signed):
    first = (lax.broadcasted_iota(jnp.int32, x.shape, 1) % 32) < 16
    partner = jnp.where(first, pltpu.roll(x, LANE - 16, 1), pltpu.roll(x, 16, 1))
    return x * cos + partner * sin_signed


def _lambda(lamp_ref, lam_init):
    l1 = jnp.sum(lamp_ref[0:1, :] * lamp_ref[1:2, :], axis=-1, keepdims=True)
    l2 = jnp.sum(lamp_ref[2:3, :] * lamp_ref[3:4, :], axis=-1, keepdims=True)
    return jnp.exp(l1) - jnp.exp(l2) + lam_init


def _softmax(s):
    e = jnp.exp(s - jnp.max(s, axis=-1, keepdims=True))
    return e / jnp.sum(e, axis=-1, keepdims=True)


def _diff_attn_head(qh, kh, vh, lam, sg, lam_init):
    lo = lax.broadcasted_iota(jnp.int32, qh.shape, 1) < A_DIM
    qs = qh * (A_DIM ** -0.5)
    q0 = jnp.where(lo, qs, 0.0).astype(BF16)
    q1 = jnp.where(lo, 0.0, qs).astype(BF16)
    a0 = _softmax(_dot_nt(q0, kh))
    a1 = _softmax(_dot_nt(q1, kh))
    w = (a0 - lam * a1).astype(BF16)
    o = _dot(w, vh)
    ms = jnp.mean(o * o, axis=-1, keepdims=True)
    return o * lax.rsqrt(ms + EPS) * sg * (1.0 - lam_init)


def _attn_ctx_kernel(*refs, lam_init, n_carried):
    lamp_ref, q_ref, k_ref, v_ref, qg_ref, kg_ref, sg_ref = refs[:7]
    o_ref, kn_ref, vc_ref = refs[7 + n_carried:]
    lam = _lambda(lamp_ref, lam_init)
    n_slots = kn_ref.shape[1]
    for s in range(n_slots):
        vc_ref[0, s] = v_ref[...]
    for h in range(A_HEADS):
        sl = slice(h * LANE, (h + 1) * LANE)
        kn = _qk_norm(k_ref[:, sl], kg_ref[...])
        for s in range(n_slots):
            kn_ref[0, s, :, sl] = kn
        qn = _qk_norm(q_ref[:, sl], qg_ref[...])
        o = _diff_attn_head(qn, kn.astype(BF16), v_ref[:, sl].astype(BF16), lam, sg_ref[...], lam_init)
        o_ref[:, sl] = o.astype(BF16)


def _attn_ctx_call(proj, lamp, qg2, kg2, sg, lam_init, layer, carried):
    blk = lambda c: pl.BlockSpec((SEQ, D), lambda b, c=c: (b, c))
    vec = pl.BlockSpec((1, LANE), lambda b: (0, 0))
    if carried:
        cache = pl.BlockSpec((1, 1, SEQ, D), lambda b: (b, layer, 0, 0))
    else:
        cache = pl.BlockSpec((1, DEPTH, SEQ, D), lambda b: (b, 0, 0, 0))
    cache_shape = jax.ShapeDtypeStruct((BATCH, DEPTH, SEQ, D), F32)
    n_in = 7
    return pl.pallas_call(
        functools.partial(_attn_ctx_kernel, lam_init=lam_init, n_carried=len(carried)),
        out_shape=(jax.ShapeDtypeStruct((TP, D), BF16), cache_shape, cache_shape),
        grid=(BATCH,),
        in_specs=[pl.BlockSpec((4, A_DIM), lambda b: (0, 0)), blk(0), blk(1), blk(2), vec, vec, vec]
                 + [pl.BlockSpec(memory_space=pl.ANY)] * len(carried),
        out_specs=(pl.BlockSpec((SEQ, D), lambda b: (b, 0)), cache, cache),
        input_output_aliases={n_in + j: 1 + j for j in range(len(carried))},
        compiler_params=_params(1, VMEM_LIMIT),
        name="attn_ctx",
    )(lamp, proj, proj, proj, qg2, kg2, sg, *carried)


def _attn_lat_kernel(lamp_ref, q_ref, k_ref, v_ref, kc_ref, vc_ref, cos_ref, sin_ref, cosq_ref, sinq_ref,
                     qg_ref, kg_ref, sg_ref, o_ref, kall_ref, vall_ref, *, lam_init):
    @pl.when(pl.program_id(1) == 0)
    def _():
        for h in range(A_HEADS):
            sl = slice(h * LANE, (h + 1) * LANE)
            kn = _rope(_qk_norm(k_ref[:, sl], kg_ref[...]), cos_ref[...], sin_ref[...])
            kall_ref[0:DEC_SEQ, sl] = kn.astype(BF16)
        vall_ref[0:DEC_SEQ, :] = v_ref[...].astype(BF16)
        kall_ref[DEC_SEQ:, :] = kc_ref[0, 0].astype(BF16)
        vall_ref[DEC_SEQ:, :] = vc_ref[0, 0].astype(BF16)

    lam = _lambda(lamp_ref, lam_init)
    for h in range(A_HEADS):
        sl = slice(h * LANE, (h + 1) * LANE)
        qn = _rope(_qk_norm(q_ref[:, sl], qg_ref[...]), cosq_ref[...], sinq_ref[...])
        o = _diff_attn_head(qn, kall_ref[:, sl], vall_ref[:, sl], lam, sg_ref[...], lam_init)
        o_ref[:, sl] = o.astype(BF16)


def _attn_lat_call(proj, cache_k, cache_v, layer, rope_cos, rope_sin, lamp, qg2, kg2, sg, lam_init):
    nq = DEC_SEQ // TQ
    r0 = TP // DEC_SEQ
    vec = pl.BlockSpec((1, LANE), lambda b, i: (0, 0))
    ctx = pl.BlockSpec((1, 1, PAST_LEN, D), lambda b, i: (b, layer, 0, 0))
    full = lambda c: pl.BlockSpec((DEC_SEQ, D), lambda b, i, c=c: (r0 + b, c))
    return pl.pallas_call(
        functools.partial(_attn_lat_kernel, lam_init=lam_init),
        out_shape=jax.ShapeDtypeStruct((TS, D), BF16),
        grid=(DEC_BATCH, nq),
        in_specs=[pl.BlockSpec((4, A_DIM), lambda b, i: (0, 0)),
                  pl.BlockSpec((TQ, D), lambda b, i: ((TP // TQ) + b * nq + i, 0)),
                  full(1), full(2), ctx, ctx,
                  pl.BlockSpec((DEC_SEQ, LANE), lambda b, i: (0, 0)),
                  pl.BlockSpec((DEC_SEQ, LANE), lambda b, i: (0, 0)),
                  pl.BlockSpec((TQ, LANE), lambda b, i: (i, 0)),
                  pl.BlockSpec((TQ, LANE), lambda b, i: (i, 0)),
                  vec, vec, vec],
        out_specs=pl.BlockSpec((TQ, D), lambda b, i: (b * nq + i, 0)),
        scratch_shapes=[pltpu.VMEM((DEC_SEQ + PAST_LEN, D), BF16),
                        pltpu.VMEM((DEC_SEQ + PAST_LEN, D), BF16)],
        compiler_params=_params(2, VMEM_LIMIT),
        name="attn_lat",
    )(lamp, proj, proj, proj, cache_k, cache_v, rope_cos, rope_sin, rope_cos, rope_sin, qg2, kg2, sg)


def _rope_tables():
    t = np.arange(DEC_SEQ)
    row, col = t // GRID_W, t % GRID_W
    lane = np.arange(LANE)
    jj = lane % A_DIM
    freq = ROPE_BASE ** (-(jj % 16).astype(np.float64) / 16.0)
    pos = np.where((jj // 32)[None, :] == 0, row[:, None], col[:, None]).astype(np.float64)
    ang = pos.astype(np.float32) * freq.astype(np.float32)[None, :]
    sign = np.where((jj % 32) < 16, -1.0, 1.0).astype(np.float32)
    return jnp.cos(jnp.asarray(ang)), jnp.sin(jnp.asarray(ang)) * jnp.asarray(sign)[None, :]


def _conv_neighbours(i):
    j = (i - TP // CONV_TILE) % (DEC_SEQ // CONV_TILE)
    lat = i >= TP // CONV_TILE
    return lat & (j > 0), lat & (j < DEC_SEQ // CONV_TILE - 1)


def _conv_kernel(a_ref, gate_ref, ap_ref, gp_ref, an_ref, gn_ref, w_ref, b_ref, g_ref, beta_ref, o_ref,
                 hp_ref, hs_ref, acc_ref):
    halo = CONV_HALO
    seq = CONV_TILE
    has_prev, has_next = _conv_neighbours(pl.program_id(0))
    hp_ref[0:halo, :] = jnp.where(has_prev, ap_ref[...] * _sigmoid(gp_ref[...]), 0.0)
    hp_ref[halo + seq:, :] = jnp.where(has_next, an_ref[...] * _sigmoid(gn_ref[...]), 0.0)
    hp_ref[halo:halo + seq, :] = a_ref[...] * _sigmoid(gate_ref[...])

    srows = hs_ref.shape[1]
    for s in range(SUBLANE):
        hs_ref[s] = hp_ref[s:s + srows, :]

    for t0 in range(0, seq, CONV_ROWS):
        for c0 in range(0, D, CONV_COLS):
            acc = jnp.zeros((CONV_ROWS, CONV_COLS), F32)
            for j in range(CONV_WIDTH):
                r = halo - CONV_PAD + t0 + j
                s = r % SUBLANE
                acc = acc + (w_ref[j:j + 1, c0:c0 + CONV_COLS]
                             * hs_ref[s, r - s:r - s + CONV_ROWS, c0:c0 + CONV_COLS])
            acc_ref[t0:t0 + CONV_ROWS, c0:c0 + CONV_COLS] = acc + b_ref[:, c0:c0 + CONV_COLS]

    h = acc_ref[...]
    mu = jnp.mean(h, axis=-1, keepdims=True)
    hc = h - mu
    y = hc * lax.rsqrt(jnp.mean(hc * hc, axis=-1, keepdims=True) + EPS) * g_ref[...] + beta_ref[...]
    o_ref[...] = (y * _sigmoid(y)).astype(BF16)


def _conv_call(proj, conv_w, conv_b, cln_g, cln_b):
    vec = pl.BlockSpec((1, D), lambda i: (0, 0))
    per = CONV_TILE // CONV_HALO
    cur = lambda c: pl.BlockSpec((CONV_TILE, D), lambda i, c=c: (i, c))
    prev = lambda c: pl.BlockSpec((CONV_HALO, D), lambda i, c=c: (jnp.maximum(i * per - 1, 0), c))
    nxt = lambda c: pl.BlockSpec((CONV_HALO, D), lambda i, c=c: (jnp.minimum((i + 1) * per, T // CONV_HALO - 1), c))
    return pl.pallas_call(
        _conv_kernel,
        out_shape=jax.ShapeDtypeStruct((T, D), BF16),
        grid=(T // CONV_TILE,),
        in_specs=[cur(3), cur(4), prev(3), prev(4), nxt(3), nxt(4),
                  pl.BlockSpec((CONV_WIDTH, D), lambda i: (0, 0)),
                  vec, vec, vec],
        out_specs=pl.BlockSpec((CONV_TILE, D), lambda i: (i, 0)),
        scratch_shapes=[pltpu.VMEM((CONV_TILE + 2 * CONV_HALO, D), F32),
                        pltpu.VMEM((SUBLANE, CONV_TILE + 2 * CONV_HALO - SUBLANE, D), F32),
                        pltpu.VMEM((CONV_TILE, D), F32)],
        compiler_params=_params(1, VMEM_LIMIT),
        name="conformer_conv",
    )(proj, proj, proj, proj, proj, proj, conv_w, conv_b, cln_g, cln_b)


def _mlstm_kernel(*refs, seq, has_state, emit_state, layer, n_carried):
    if has_state:
        (m0_ref, q_ref, k_ref, v_ref, og_ref, gc_ref, gr_ref, mg_ref, c0_ref, n0_ref) = refs[:10]
        rest = refs[10:]
    else:
        (q_ref, k_ref, v_ref, og_ref, gc_ref, gr_ref, mg_ref) = refs[:7]
        rest = refs[7 + n_carried:]
    if emit_state:
        o_ref, cout_ref, nm_ref, hacc_ref, c_ref = rest
    else:
        o_ref, hacc_ref, c_ref = rest

    b_id = pl.program_id(0)
    h_id = pl.program_id(1)
    L = M_CHUNK
    nc = seq // L
    t_idx = lax.broadcasted_iota(jnp.int32, (L, L), 0)
    s_idx = lax.broadcasted_iota(jnp.int32, (L, L), 1)

    for d in range(2):
        mask = (s_idx <= t_idx) if d == 0 else (s_idx >= t_idx)
        mask_t = (t_idx <= s_idx) if d == 0 else (t_idx >= s_idx)
        last = L - 1 if d == 0 else 0
        if has_state:
            c_ref[...] = c0_ref[0, 0, d, 0]
            n = n0_ref[0, 0, d, pl.ds(h_id, 1), :]
            m = jnp.full((1, 1), m0_ref[b_id, layer, d, h_id], F32)
        else:
            n = jnp.zeros((1, M_DIM), F32)
            m = jnp.zeros((1, 1), F32)
        chunks = range(nc) if d == 0 else range(nc - 1, -1, -1)
        for ci, c in enumerate(chunks):
            rows = slice(c * L, (c + 1) * L)
            first_chunk = ci == 0
            last_chunk = ci == nc - 1
            q = q_ref[rows, :]
            k = k_ref[rows, :] * (M_DIM ** -0.5)
            qb, kb, vb = q.astype(BF16), k.astype(BF16), v_ref[rows, :].astype(BF16)
            gi_col = gc_ref[0, rows, d:d + 1]
            gi_row = gr_ref[0, d:d + 1, rows]
            lf_col = _log_sigmoid(gc_ref[0, rows, 2 + d:3 + d])
            lf_row = _log_sigmoid(gr_ref[0, 2 + d:3 + d, rows])
            b_col = jnp.sum(jnp.where(mask, lf_row, 0.0), axis=1, keepdims=True)
            b_row = jnp.sum(jnp.where(mask_t, lf_col, 0.0), axis=0, keepdims=True)
            dmat = jnp.where(mask, b_col - b_row + gi_row, -jnp.inf)
            inter = b_col + m
            mt = jnp.maximum(inter, jnp.max(dmat, axis=1, keepdims=True))
            w = jnp.exp(dmat - mt)
            s_inter = jnp.exp(inter - mt)
            sw = _dot_nt(qb, kb) * w
            num = _dot(sw.astype(BF16), vb)
            den = jnp.sum(sw, axis=1, keepdims=True)
            if has_state or not first_chunk:
                num = num + s_inter * _dot(qb, c_ref[...].astype(BF16))
                den = den + s_inter * jnp.sum(q * n, axis=1, keepdims=True)
            hout = num / jnp.maximum(jnp.abs(den), jnp.exp(-mt))
            if d == 0:
                hacc_ref[rows, :] = hout
            else:
                hacc_ref[rows, :] = hacc_ref[rows, :] + hout
            if emit_state or not last_chunk:
                m_new = mt[last:last + 1, :]
                btot = b_col[last:last + 1, :]
                decay = jnp.exp(btot + m - m_new)
                wk_col = jnp.exp(btot - b_col + gi_col - m_new)
                kw = k * wk_col
                upd = _dot_tn(kw.astype(BF16), vb)
                nsum = jnp.sum(kw, axis=0, keepdims=True)
                if has_state or not first_chunk:
                    c_ref[...] = decay * c_ref[...] + upd
                    n = decay * n + nsum
                else:
                    c_ref[...] = upd
                    n = nsum
                m = m_new
        if emit_state:
            for s in range(cout_ref.shape[1]):
                cout_ref[0, s, d, 0] = c_ref[...]
            nm_ref[0, 0, d:d + 1, 0:M_DIM] = n
            nm_ref[0, 0, d:d + 1, M_DIM:] = jnp.broadcast_to(m, (1, LANE))

    hm = hacc_ref[...]
    ms = jnp.mean(hm * hm, axis=-1, keepdims=True)
    y = hm * lax.rsqrt(ms + EPS) * mg_ref[...]
    o_ref[...] = (y * _sigmoid(og_ref[...])).astype(BF16)


def _mlstm_call(proj, gates_col, gates_row, mnorm_g, seq, nb, row0, layer, states=None, carried=()):
    has_state = states is not None
    emit_state = not has_state
    aliases = {}
    col = lambda c0: pl.BlockSpec((seq, M_DIM), lambda b, h, c0=c0: (row0 + b, c0 + h))
    in_specs = [col(20), col(24), col(28), col(32),
                pl.BlockSpec((1, seq, 4), lambda b, h: (h, row0 + b, 0)),
                pl.BlockSpec((1, 4, seq), lambda b, h: (h, 0, row0 + b)),
                pl.BlockSpec((1, M_DIM), lambda b, h: (0, h))]
    args = [proj, proj, proj, proj, gates_col, gates_row, mnorm_g]
    if has_state:
        state_c, state_n, state_m = states
        in_specs = [pl.BlockSpec(memory_space=pltpu.SMEM)] + in_specs + [
            pl.BlockSpec((1, 1, 2, 1, M_DIM, M_DIM), lambda b, h: (b, layer, 0, h, 0, 0)),
            pl.BlockSpec((1, 1, 2, M_HEADS, M_DIM), lambda b, h: (b, layer, 0, 0, 0))]
        args = [state_m] + args + [state_c, state_n]
    out_shape = [jax.ShapeDtypeStruct((nb * seq, D), BF16)]
    out_specs = [pl.BlockSpec((seq, M_DIM), lambda b, h: (b, h))]
    if emit_state:
        out_shape += [jax.ShapeDtypeStruct((nb, DEPTH, 2, M_HEADS, M_DIM, M_DIM), F32),
                      jax.ShapeDtypeStruct((nb, M_HEADS, 2, M_DIM + LANE), F32)]
        if carried:
            c_spec = pl.BlockSpec((1, 1, 2, 1, M_DIM, M_DIM), lambda b, h: (b, layer, 0, h, 0, 0))
        else:
            c_spec = pl.BlockSpec((1, DEPTH, 2, 1, M_DIM, M_DIM), lambda b, h: (b, 0, 0, h, 0, 0))
        out_specs += [c_spec, pl.BlockSpec((1, 1, 2, M_DIM + LANE), lambda b, h: (b, h, 0, 0))]
        aliases = {len(args) + j: 1 + j for j in range(len(carried))}
        in_specs = in_specs + [pl.BlockSpec(memory_space=pl.ANY)] * len(carried)
        args = args + list(carried)
    return pl.pallas_call(
        functools.partial(_mlstm_kernel, seq=seq, has_state=has_state, emit_state=emit_state, layer=layer,
                          n_carried=len(carried)),
        out_shape=tuple(out_shape),
        grid=(nb, M_HEADS),
        in_specs=in_specs,
        out_specs=tuple(out_specs),
        input_output_aliases=aliases,
        scratch_shapes=[pltpu.VMEM((seq, M_DIM), F32), pltpu.VMEM((M_DIM, M_DIM), F32)],
        compiler_params=_params(2, VMEM_LIMIT),
        name="mlstm",
    )(*args)


def _mod_row_merge(i):
    per = DEC_SEQ // TM_MERGE
    return jnp.where(i < TP // TM_MERGE, 0, 1 + (i - TP // TM_MERGE) // per)


def _split3(x):
    hi = x.astype(BF16)
    lo = (x - hi.astype(F32)).astype(BF16)
    return hi, lo


def _merge_kernel(oap, oas, ob_ref, ocp, ocs, gl_ref, x_ref, mod_ref, wa_ref, wb_ref, wc_ref, wo_ref,
                  g2_ref, rwh_ref, rwl_ref, rb_ref, x1_ref, xn_ref, route_ref, cnt_ref, carry_ref):
    i = pl.program_id(0)
    is_ctx = i < TP // TM_MERGE

    @pl.when(i == 0)
    def _():
        carry_ref[...] = jnp.zeros_like(carry_ref)

    def branch(p_ref, s_ref, w_ref, c):
        o = jnp.where(is_ctx, p_ref[...], s_ref[...])
        return _sigmoid(gl_ref[:, c * D:(c + 1) * D]) * _dot(o, w_ref[...])

    merged = (branch(oap, oas, wa_ref, 0)
              + _sigmoid(gl_ref[:, D:2 * D]) * _dot(ob_ref[...], wb_ref[...])
              + branch(ocp, ocs, wc_ref, 2))
    out = _dot(merged.astype(BF16), wo_ref[...])
    x1 = x_ref[...] + mod_ref[0, 2:3, :] * out
    x1_ref[...] = x1
    ms = jnp.mean(x1 * x1, axis=-1, keepdims=True)
    xn = x1 * lax.rsqrt(ms + EPS) * g2_ref[...] * (1.0 + mod_ref[0, 4:5, :]) + mod_ref[0, 3:4, :]
    xn_ref[...] = xn

    xh, xl = _split3(xn)
    logits = _dot(xh, rwh_ref[...]) + _dot(xh, rwl_ref[...]) + _dot(xl, rwh_ref[...]) + rb_ref[...]
    lane = lax.broadcasted_iota(jnp.int32, logits.shape, 1)
    lanef = lane.astype(F32)
    cur = jnp.where(lane < N_EXPERTS, logits, -jnp.inf)
    vals, idxs = [], []
    for _ in range(TOP_K):
        mx = jnp.max(cur, axis=-1, keepdims=True)
        idx = jnp.min(jnp.where(cur == mx, lanef, float(LANE)), axis=-1, keepdims=True)
        vals.append(mx)
        idxs.append(idx)
        cur = jnp.where(lanef == idx, -jnp.inf, cur)
    exps = [jnp.exp(v - vals[0]) for v in vals]
    den = exps[0] + exps[1] + exps[2] + exps[3]

    onehot = jnp.zeros(logits.shape, F32)
    for idx in idxs:
        onehot = onehot + jnp.where(lanef == idx, 1.0, 0.0)
    r_idx = lax.broadcasted_iota(jnp.int32, (TM_MERGE, TM_MERGE), 0)
    c_idx = lax.broadcasted_iota(jnp.int32, (TM_MERGE, TM_MERGE), 1)
    tril = jnp.where(c_idx <= r_idx, 1.0, 0.0).astype(BF16)
    incl = _dot(tril, onehot.astype(BF16))
    tot = carry_ref[...] + incl
    route = jnp.zeros(logits.shape, F32)
    for k in range(TOP_K):
        rank = jnp.sum(jnp.where(lanef == idxs[k], tot, 0.0), axis=-1, keepdims=True) - 1.0
        route = jnp.where(lane == k, idxs[k], route)
        route = jnp.where(lane == TOP_K + k, exps[k] / den, route)
        route = jnp.where(lane == 2 * TOP_K + k, rank, route)
    route_ref[...] = route
    carry_ref[...] = tot[TM_MERGE - 1:TM_MERGE, :]
    cnt_ref[...] = jnp.broadcast_to(tot[TM_MERGE - 1:TM_MERGE, :], cnt_ref.shape)


def _merge_call(oa, ob, oc, proj, x, mod, wa, wb, wc, wo, g2, rwh, rwl, rb):
    npt = TP // TM_MERGE
    pblk = pl.BlockSpec((TM_MERGE, D), lambda i: (jnp.minimum(i, npt - 1), 0))
    sblk = pl.BlockSpec((TM_MERGE, D), lambda i: (jnp.maximum(i - npt, 0), 0))
    wblk = pl.BlockSpec((D, D), lambda i: (0, 0))
    rblk = pl.BlockSpec((D, LANE), lambda i: (0, 0))
    return pl.pallas_call(
        _merge_kernel,
        out_shape=(jax.ShapeDtypeStruct((T, D), F32),
                   jax.ShapeDtypeStruct((T, D), F32),
                   jax.ShapeDtypeStruct((T, LANE), F32),
                   jax.ShapeDtypeStruct((SUBLANE, LANE), F32)),
        grid=(T // TM_MERGE,),
        in_specs=[pblk, sblk, pl.BlockSpec((TM_MERGE, D), lambda i: (i, 0)), pblk, sblk,
                  pl.BlockSpec((TM_MERGE, 3 * D), lambda i: (i, 3)),
                  pl.BlockSpec((TM_MERGE, D), lambda i: (i, 0)),
                  pl.BlockSpec((1, N_MOD, D), lambda i: (_mod_row_merge(i), 0, 0)),
                  wblk, wblk, wblk, wblk,
                  pl.BlockSpec((1, D), lambda i: (0, 0)),
                  rblk, rblk,
                  pl.BlockSpec((1, LANE), lambda i: (0, 0))],
        out_specs=(pl.BlockSpec((TM_MERGE, D), lambda i: (i, 0)),
                   pl.BlockSpec((TM_MERGE, D), lambda i: (i, 0)),
                   pl.BlockSpec((TM_MERGE, LANE), lambda i: (i, 0)),
                   pl.BlockSpec((SUBLANE, LANE), lambda i: (0, 0))),
        scratch_shapes=[pltpu.VMEM((1, LANE), F32)],
        compiler_params=_params(1, VMEM_LIMIT),
        name="merge_router",
    )(oa[0], oa[1], ob, oc[0], oc[1], proj, x, mod, wa, wb, wc, wo, g2, rwh, rwl, rb)


def _row_copy(src_ref, src_row, dst_ref, dst_row, sem):
    return pltpu.make_async_copy(src_ref.at[pl.ds(src_row, 1), :], dst_ref.at[pl.ds(dst_row, 1), :], sem)


def _block_wait(hbm_ref, vmem_ref, sem, to_vmem):
    hbm_blk = hbm_ref.at[pl.ds(0, MOE_BM), :]
    (pltpu.make_async_copy(hbm_blk, vmem_ref, sem) if to_vmem
     else pltpu.make_async_copy(vmem_ref, hbm_blk, sem)).wait()


def _gmm_kernel(be_ref, src_cur_ref, src_next_ref, dst_prev_ref, dst_cur_ref, xn_ref,
                wgu_ref, bgu_ref, wdn_ref, bdn_ref, ys_ref, xa, xb, ya, yb, wgu_bf, wdn_bf, sems):
    i = pl.program_id(0)
    last = pl.num_programs(0) - 1
    prev = be_ref[jnp.maximum(i - 1, 0)]

    @pl.when((i == 0) | (be_ref[i] != prev))
    def _():
        wgu_bf[...] = wgu_ref[0, 0].astype(BF16)
        wdn_bf[...] = wdn_ref[0, 0].astype(BF16)

    @pl.when(i == 0)
    def _():
        ya[...] = jnp.zeros_like(ya)
        yb[...] = jnp.zeros_like(yb)

        def gather(r, carry):
            _row_copy(xn_ref, src_cur_ref[0, 0, r], xa, r, sems.at[0]).start()
            return carry

        lax.fori_loop(0, MOE_BM, gather, 0)
        pltpu.make_async_copy(ya, ys_ref.at[pl.ds(MOE_DUMP, MOE_BM), :], sems.at[2]).start()

    def step(x_cur, x_next, y_cur, y_prev, g_cur, g_next, s_cur, s_prev):
        _block_wait(xn_ref, x_cur, g_cur, True)
        _block_wait(ys_ref, y_cur, s_cur, False)
        for r in range(MOE_BM):
            _row_copy(xn_ref, src_next_ref[0, 0, r], x_next, r, g_next).start()
            _row_copy(y_prev, r, ys_ref, dst_prev_ref[0, 0, r], s_prev).start()
        gu = _dot(x_cur[...].astype(BF16), wgu_bf[...]) + bgu_ref[0, 0]
        g = jnp.minimum(gu[:, :D], SWIGLU_LIMIT)
        u = jnp.clip(gu[:, D:], -SWIGLU_LIMIT, SWIGLU_LIMIT)
        hdn = (u + 1.0) * (g * _sigmoid(SWIGLU_ALPHA * g))
        y_cur[...] = _dot(hdn.astype(BF16), wdn_bf[...]) + bdn_ref[0, 0]

        @pl.when(i == last)
        def _():
            _block_wait(xn_ref, x_next, g_next, True)
            _block_wait(ys_ref, y_prev, s_prev, False)

            def scatter(r, carry):
                _row_copy(y_cur, r, ys_ref, dst_cur_ref[0, 0, r], s_cur).start()
                return carry

            lax.fori_loop(0, MOE_BM, scatter, 0)
            _block_wait(ys_ref, y_cur, s_cur, False)

    @pl.when(i % 2 == 0)
    def _():
        step(xa, xb, ya, yb, sems.at[0], sems.at[1], sems.at[2], sems.at[3])

    @pl.when(i % 2 == 1)
    def _():
        step(xb, xa, yb, ya, sems.at[1], sems.at[0], sems.at[3], sems.at[2])


def _gmm_call(block_e, row_src, row_dst, xn, w_gu, b_gu, w_dn, b_dn, layer):
    idx = lambda f: pl.BlockSpec((1, 1, MOE_BM), lambda i, be: (f(i), 0, 0), memory_space=pltpu.SMEM)
    grid_spec = pltpu.PrefetchScalarGridSpec(
        num_scalar_prefetch=1,
        grid=(MOE_NB,),
        in_specs=[idx(lambda i: i), idx(lambda i: jnp.minimum(i + 1, MOE_NB - 1)),
                  idx(lambda i: i), idx(lambda i: i + 1),
                  pl.BlockSpec(memory_space=pl.ANY),
                  pl.BlockSpec((1, 1, D, 2 * D), lambda i, be: (layer, be[i], 0, 0)),
                  pl.BlockSpec((1, 1, 1, 2 * D), lambda i, be: (layer, be[i], 0, 0)),
                  pl.BlockSpec((1, 1, D, D), lambda i, be: (layer, be[i], 0, 0)),
                  pl.BlockSpec((1, 1, 1, D), lambda i, be: (layer, be[i], 0, 0))],
        out_specs=pl.BlockSpec(memory_space=pl.ANY),
        scratch_shapes=[pltpu.VMEM((MOE_BM, D), F32)] * 4
                       + [pltpu.VMEM((D, 2 * D), BF16), pltpu.VMEM((D, D), BF16),
                          pltpu.SemaphoreType.DMA((4,))],
    )
    return pl.pallas_call(
        _gmm_kernel,
        out_shape=jax.ShapeDtypeStruct((MOE_DUMP + MOE_BM, D), F32),
        grid_spec=grid_spec,
        compiler_params=_params(1, VMEM_LIMIT),
        name="moe_experts",
    )(block_e, row_src, row_src, row_dst, row_dst, xn, w_gu, b_gu.reshape(DEPTH, N_EXPERTS, 1, 2 * D), w_dn,
      b_dn.reshape(DEPTH, N_EXPERTS, 1, D))


def _mod_row_tok(i):
    per = DEC_SEQ // TOK_TILE
    return jnp.where(i < TP // TOK_TILE, 0, 1 + (i - TP // TOK_TILE) // per)


def _combine_kernel(y0_ref, y1_ref, y2_ref, y3_ref, x1_ref, route_ref, mod_ref, o_ref):
    acc = jnp.zeros((TOK_TILE, D), F32)
    for k, y_ref in enumerate((y0_ref, y1_ref, y2_ref, y3_ref)):
        acc = acc + route_ref[:, TOP_K + k:TOP_K + k + 1] * y_ref[...]
    o_ref[...] = x1_ref[...] + mod_ref[0, 5:6, :] * acc


def _combine_call(ys, x1, route, mod):
    slot = lambda k: pl.BlockSpec((TOK_TILE, D), lambda i, k=k: (k * (T // TOK_TILE) + i, 0))
    return pl.pallas_call(
        _combine_kernel,
        out_shape=jax.ShapeDtypeStruct((T, D), F32),
        grid=(T // TOK_TILE,),
        in_specs=[slot(0), slot(1), slot(2), slot(3),
                  pl.BlockSpec((TOK_TILE, D), lambda i: (i, 0)),
                  pl.BlockSpec((TOK_TILE, LANE), lambda i: (i, 0)),
                  pl.BlockSpec((1, N_MOD, D), lambda i: (_mod_row_tok(i), 0, 0))],
        out_specs=pl.BlockSpec((TOK_TILE, D), lambda i: (i, 0)),
        compiler_params=_params(1, VMEM_LIMIT),
        name="moe_combine",
    )(ys, ys, ys, ys, x1, route, mod)


def _routing_tables(route, counts):
    top_e = route[:, 0:TOP_K].astype(jnp.int32)
    rank = route[:, 2 * TOP_K:3 * TOP_K].astype(jnp.int32)
    cnt = counts[0, :N_EXPERTS].astype(jnp.int32)
    padded = (cnt + MOE_BM - 1) // MOE_BM * MOE_BM
    pad_end = jnp.cumsum(padded)
    pad_start = pad_end - padded
    pos = (pad_start[top_e] + rank).reshape(-1)
    starts = jnp.arange(MOE_NB, dtype=jnp.int32) * MOE_BM
    block_e = jnp.minimum(jnp.sum((pad_end[None, :] <= starts[:, None]).astype(jnp.int32), axis=1),
                          N_EXPERTS - 1)
    tok = jnp.broadcast_to(jnp.arange(T, dtype=jnp.int32)[:, None], (T, TOP_K))
    slot = jnp.arange(TOP_K, dtype=jnp.int32)[None, :] * T + tok
    row_src = jnp.zeros((MOE_ROWS,), jnp.int32).at[pos].set(tok.reshape(-1), unique_indices=True)
    dump = MOE_DUMP + jnp.arange(MOE_ROWS, dtype=jnp.int32) % MOE_BM
    row_dst = dump.at[pos].set(slot.reshape(-1), unique_indices=True)
    row_dst = jnp.concatenate([dump[:MOE_BM], row_dst])
    return block_e, row_src.reshape(MOE_NB, 1, MOE_BM), row_dst.reshape(MOE_NB + 1, 1, MOE_BM)


def kernel(x_prompt, x_sample, c, c_ctx, cache_k, cache_v, state_C, state_n, state_m, ada_w, ada_b, norm1_g, norm2_g, w_in, b_in, qn_g, kn_g, lam_q1, lam_k1, lam_q2, lam_k2, subln_g, conv_w, conv_b, cln_g, cln_b, mnorm_g, w_br_a, w_br_b, w_br_c, w_out, router_w, router_b, w_gu, b_gu, w_dn, b_dn):
    x = jnp.concatenate([x_prompt.reshape(TP, D), x_sample.reshape(TS, D)], axis=0)
    cond = jnp.concatenate([c_ctx[None, :], c, jnp.zeros((SUBLANE - 1 - DEC_BATCH, D), F32)], axis=0)
    mod_all = _ada_call(cond, ada_w, ada_b).reshape(DEPTH, SUBLANE, N_MOD, D)
    rope_cos, rope_sin = _rope_tables()
    cache_k4 = cache_k.reshape(DEC_BATCH, DEPTH, PAST_LEN, D)
    cache_v4 = cache_v.reshape(DEC_BATCH, DEPTH, PAST_LEN, D)

    kv_cache, c_state, ns, ms = (), (), [], []
    for l in range(DEPTH):
        lam_init = 0.8 - 0.6 * math.exp(-0.3 * l)
        mod = mod_all[l]
        w = w_in[l]
        w_main = jnp.concatenate([w[:, :GATE_OFF], w[:, GL_OFF:]], axis=1).astype(BF16)
        b_main = jnp.concatenate([b_in[l, :GATE_OFF], b_in[l, GL_OFF:]])[None, :]
        w_gate = jnp.pad(w[:, GATE_OFF:GL_OFF], ((0, 0), (0, LANE - 16))).astype(BF16)
        b_gate = jnp.pad(b_in[l, GATE_OFF:GL_OFF], (0, LANE - 16))[None, :]
        proj, gates = _inproj_call(x, mod, norm1_g[l][None, :], w_main, b_main, w_gate, b_gate)

        g4 = gates[:, :16].reshape(T, 2, 2, M_HEADS)
        gates_col = jnp.transpose(g4, (3, 0, 1, 2)).reshape(M_HEADS, T, 4)
        gates_row = jnp.transpose(g4, (3, 1, 2, 0)).reshape(M_HEADS, 4, T)

        lamp = jnp.stack([lam_q1[l], lam_k1[l], lam_q2[l], lam_k2[l]])
        qg2 = jnp.tile(qn_g[l], 2)[None, :]
        kg2 = jnp.tile(kn_g[l], 2)[None, :]
        sg = subln_g[l][None, :]
        oa_p, *kv_cache = _attn_ctx_call(proj, lamp, qg2, kg2, sg, lam_init, l, tuple(kv_cache))
        oa_s = _attn_lat_call(proj, cache_k4, cache_v4, l, rope_cos, rope_sin, lamp, qg2, kg2, sg, lam_init)

        cw, cb, cg, cbeta = conv_w[l], conv_b[l][None, :], cln_g[l][None, :], cln_b[l][None, :]
        ob = _conv_call(proj, cw, cb, cg, cbeta)

        mg = mnorm_g[l][None, :]
        oc_p, c_new, nm_new = _mlstm_call(proj, gates_col, gates_row, mg, SEQ, BATCH, 0, l, carried=c_state)
        c_state = (c_new,)
        (oc_s,) = _mlstm_call(proj, gates_col, gates_row, mg, DEC_SEQ, DEC_BATCH, TP // DEC_SEQ, l,
                              states=(state_C, state_n, state_m))

        rw = jnp.pad(router_w[l], ((0, 0), (0, LANE - N_EXPERTS)))
        rwh = rw.astype(BF16)
        rwl = (rw - rwh.astype(F32)).astype(BF16)
        rb = jnp.pad(router_b[l], (0, LANE - N_EXPERTS))[None, :]
        x1, xn2, route, counts = _merge_call(
            (oa_p, oa_s), ob, (oc_p, oc_s), proj, x, mod,
            w_br_a[l].astype(BF16), w_br_b[l].astype(BF16), w_br_c[l].astype(BF16), w_out[l].astype(BF16),
            norm2_g[l][None, :], rwh, rwl, rb)

        block_e, row_src, row_dst = _routing_tables(route, counts)
        ys = _gmm_call(block_e, row_src, row_dst, xn2, w_gu, b_gu, w_dn, b_dn, l)
        x = _combine_call(ys, x1, route, mod)

        ns.append(jnp.transpose(nm_new[..., :M_DIM], (0, 2, 1, 3)))
        ms.append(jnp.transpose(nm_new[..., M_DIM], (0, 2, 1)))

    y_p = x[:TP].reshape(BATCH, SEQ, D)
    y_s = x[TP:].reshape(DEC_BATCH, DEC_SEQ, D)
    new_k = kv_cache[0].reshape(BATCH, DEPTH, SEQ, A_HEADS, 2 * A_DIM)
    new_v = kv_cache[1].reshape(BATCH, DEPTH, SEQ, A_HEADS, A_VDIM)
    return (y_p, y_s, new_k, new_v, c_state[0], jnp.stack(ns, axis=1), jnp.stack(ms, axis=1))
```

```python
import functools
import math

import jax
import jax.numpy as jnp
import numpy as np
from jax import lax
from jax.experimental import pallas as pl
from jax.experimental.pallas import tpu as pltpu

F32 = jnp.float32
BF16 = jnp.bfloat16

D = 1024
BATCH = 32
SEQ = 256
DEPTH = 2
DEC_BATCH = 4
DEC_SEQ = 1024
PAST_LEN = 512
GRID_W = 64
EPS = 1e-6
A_HEADS = 8
A_DIM = 64
A_VDIM = 128
ROPE_BASE = 10000.0
CONV_WIDTH = 31
CONV_PAD = CONV_WIDTH // 2
M_HEADS = 4
M_DIM = 256
N_EXPERTS = 32
TOP_K = 4
SWIGLU_LIMIT = 7.0
SWIGLU_ALPHA = 1.702
N_MOD = 6

TP = BATCH * SEQ
TS = DEC_BATCH * DEC_SEQ
T = TP + TS
N_MAIN = 12 * D
GATE_OFF = 9 * D
GL_OFF = GATE_OFF + 16

LANE = 128
SUBLANE = 8
VMEM_LIMIT = 56 * 1024 * 1024

TM_IN = 1024
TN_IN = 1024
TM_MERGE = 256
TQ = 256
M_CHUNK = 256
CONV_TILE = 256
CONV_HALO = 16
CONV_ROWS = 32
CONV_COLS = 512
MOE_BM = 512
MOE_NB = T * TOP_K // MOE_BM + N_EXPERTS
MOE_ROWS = MOE_NB * MOE_BM
TOK_TILE = 128


def _sigmoid(x):
    return 1.0 / (1.0 + jnp.exp(-x))


def _log_sigmoid(x):
    return jnp.minimum(x, 0.0) - jnp.log1p(jnp.exp(-jnp.abs(x)))


def _dot(a, b):
    return jnp.dot(a, b, preferred_element_type=F32)


def _dot_nt(a, b):
    return lax.dot_general(a, b, (((1,), (1,)), ((), ())), preferred_element_type=F32)


def _dot_tn(a, b):
    return lax.dot_general(a, b, (((0,), (0,)), ((), ())), preferred_element_type=F32)


def _params(n_axes, vmem=None):
    return pltpu.CompilerParams(dimension_semantics=("arbitrary",) * n_axes,
                                vmem_limit_bytes=vmem)


def _ada_kernel(c_ref, w_ref, b_ref, o_ref):
    c = c_ref[...]
    s = (c * _sigmoid(c)).astype(BF16)
    o_ref[0] = _dot(s, w_ref[0].astype(BF16)) + b_ref[0]


def _ada_call(cond, ada_w, ada_b):
    n = N_MOD * D
    return pl.pallas_call(
        _ada_kernel,
        out_shape=jax.ShapeDtypeStruct((DEPTH, SUBLANE, n), F32),
        grid=(DEPTH, n // D),
        in_specs=[pl.BlockSpec((SUBLANE, D), lambda l, j: (0, 0)),
                  pl.BlockSpec((1, D, D), lambda l, j: (l, 0, j)),
                  pl.BlockSpec((1, 1, D), lambda l, j: (l, 0, j))],
        out_specs=pl.BlockSpec((1, SUBLANE, D), lambda l, j: (l, 0, j)),
        compiler_params=_params(2),
        name="ada_mod",
    )(cond, ada_w, ada_b.reshape(DEPTH, 1, n))


def _mod_row_in(i):
    return jnp.where(i < TP // TM_IN, 0, i - (TP // TM_IN - 1))


def _inproj_kernel(x_ref, mod_ref, g_ref, w_ref, b_ref, wg_ref, bg_ref, o_ref, og_ref, xn_ref):
    @pl.when(pl.program_id(1) == 0)
    def _():
        x = x_ref[...]
        ms = jnp.mean(x * x, axis=-1, keepdims=True)
        y = x * lax.rsqrt(ms + EPS) * g_ref[...]
        xn = (y * (1.0 + mod_ref[0, 1:2, :]) + mod_ref[0, 0:1, :]).astype(BF16)
        xn_ref[...] = xn
        og_ref[...] = _dot(xn, wg_ref[...]) + bg_ref[...]

    o_ref[...] = (_dot(xn_ref[...], w_ref[...]) + b_ref[...]).astype(BF16)


def _inproj_call(x, mod, g, w_main, b_main, w_gate, b_gate):
    return pl.pallas_call(
        _inproj_kernel,
        out_shape=(jax.ShapeDtypeStruct((T, N_MAIN), BF16),
                   jax.ShapeDtypeStruct((T, LANE), F32)),
        grid=(T // TM_IN, N_MAIN // TN_IN),
        in_specs=[pl.BlockSpec((TM_IN, D), lambda i, j: (i, 0)),
                  pl.BlockSpec((1, N_MOD, D), lambda i, j: (_mod_row_in(i), 0, 0)),
                  pl.BlockSpec((1, D), lambda i, j: (0, 0)),
                  pl.BlockSpec((D, TN_IN), lambda i, j: (0, j)),
                  pl.BlockSpec((1, TN_IN), lambda i, j: (0, j)),
                  pl.BlockSpec((D, LANE), lambda i, j: (0, 0)),
                  pl.BlockSpec((1, LANE), lambda i, j: (0, 0))],
        out_specs=(pl.BlockSpec((TM_IN, TN_IN), lambda i, j: (i, j)),
                   pl.BlockSpec((TM_IN, LANE), lambda i, j: (i, 0))),
        scratch_shapes=[pltpu.VMEM((TM_IN, D), BF16)],
        compiler_params=_params(2, VMEM_LIMIT),
        name="in_proj",
    )(x, mod, g, w_main, b_main, w_gate, b_gate)


def _qk_norm(x, g2):
    lo = lax.broadcasted_iota(jnp.int32, x.shape, 1) < A_DIM
    x2 = x * x
    s0 = jnp.sum(jnp.where(lo, x2, 0.0), axis=-1, keepdims=True)
    s1 = jnp.sum(jnp.where(lo, 0.0, x2), axis=-1, keepdims=True)
    ms = jnp.where(lo, s0, s1) * (1.0 / A_DIM)
    return x * lax.rsqrt(ms + EPS) * g2


def _rope(x, cos, sin_signed):
    first = (lax.broadcasted_iota(jnp.int32, x.shape, 1) % 32) < 16
    partner = jnp.where(first, pltpu.roll(x, LANE - 16, 1), pltpu.roll(x, 16, 1))
    return x * cos + partner * sin_signed


def _lambda(lamp_ref, lam_init):
    l1 = jnp.sum(lamp_ref[0:1, :] * lamp_ref[1:2, :], axis=-1, keepdims=True)
    l2 = jnp.sum(lamp_ref[2:3, :] * lamp_ref[3:4, :], axis=-1, keepdims=True)
    return jnp.exp(l1) - jnp.exp(l2) + lam_init


def _softmax(s):
    e = jnp.exp(s - jnp.max(s, axis=-1, keepdims=True))
    return e / jnp.sum(e, axis=-1, keepdims=True)


def _diff_attn_head(qh, kh, vh, lam, sg, lam_init):
    lo = lax.broadcasted_iota(jnp.int32, qh.shape, 1) < A_DIM
    qs = qh * (A_DIM ** -0.5)
    q0 = jnp.where(lo, qs, 0.0).astype(BF16)
    q1 = jnp.where(lo, 0.0, qs).astype(BF16)
    a0 = _softmax(_dot_nt(q0, kh))
    a1 = _softmax(_dot_nt(q1, kh))
    w = (a0 - lam * a1).astype(BF16)
    o = _dot(w, vh)
    ms = jnp.mean(o * o, axis=-1, keepdims=True)
    return o * lax.rsqrt(ms + EPS) * sg * (1.0 - lam_init)


def _attn_ctx_kernel(*refs, lam_init, n_carried):
    lamp_ref, q_ref, k_ref, v_ref, qg_ref, kg_ref, sg_ref = refs[:7]
    o_ref, kn_ref, vc_ref = refs[7 + n_carried:]
    lam = _lambda(lamp_ref, lam_init)
    n_slots = kn_ref.shape[1]
    for s in range(n_slots):
        vc_ref[0, s] = v_ref[...].astype(F32)
    for h in range(A_HEADS):
        sl = slice(h * LANE, (h + 1) * LANE)
        kn = _qk_norm(k_ref[:, sl].astype(F32), kg_ref[...])
        for s in range(n_slots):
            kn_ref[0, s, :, sl] = kn
        qn = _qk_norm(q_ref[:, sl].astype(F32), qg_ref[...])
        o = _diff_attn_head(qn, kn.astype(BF16), v_ref[:, sl], lam, sg_ref[...], lam_init)
        o_ref[:, sl] = o.astype(BF16)


def _attn_ctx_call(proj, lamp, qg2, kg2, sg, lam_init, layer, carried):
    blk = lambda c: pl.BlockSpec((SEQ, D), lambda b, c=c: (b, c))
    vec = pl.BlockSpec((1, LANE), lambda b: (0, 0))
    if carried:
        cache = pl.BlockSpec((1, 1, SEQ, D), lambda b: (b, layer, 0, 0))
    else:
        cache = pl.BlockSpec((1, DEPTH, SEQ, D), lambda b: (b, 0, 0, 0))
    cache_shape = jax.ShapeDtypeStruct((BATCH, DEPTH, SEQ, D), F32)
    n_in = 7
    return pl.pallas_call(
        functools.partial(_attn_ctx_kernel, lam_init=lam_init, n_carried=len(carried)),
        out_shape=(jax.ShapeDtypeStruct((TP, D), BF16), cache_shape, cache_shape),
        grid=(BATCH,),
        in_specs=[pl.BlockSpec((4, A_DIM), lambda b: (0, 0)), blk(0), blk(1), blk(2), vec, vec, vec]
                 + [pl.BlockSpec(memory_space=pl.ANY)] * len(carried),
        out_specs=(pl.BlockSpec((SEQ, D), lambda b: (b, 0)), cache, cache),
        input_output_aliases={n_in + j: 1 + j for j in range(len(carried))},
        compiler_params=_params(1, VMEM_LIMIT),
        name="attn_ctx",
    )(lamp, proj, proj, proj, qg2, kg2, sg, *carried)


def _attn_lat_kernel(lamp_ref, q_ref, k_ref, v_ref, kc_ref, vc_ref, cos_ref, sin_ref, cosq_ref, sinq_ref,
                     qg_ref, kg_ref, sg_ref, o_ref, kall_ref, vall_ref, *, lam_init):
    @pl.when(pl.program_id(1) == 0)
    def _():
        for h in range(A_HEADS):
            sl = slice(h * LANE, (h + 1) * LANE)
            kn = _rope(_qk_norm(k_ref[:, sl].astype(F32), kg_ref[...]), cos_ref[...], sin_ref[...])
            kall_ref[0:DEC_SEQ, sl] = kn.astype(BF16)
        vall_ref[0:DEC_SEQ, :] = v_ref[...]
        kall_ref[DEC_SEQ:, :] = kc_ref[0, 0].astype(BF16)
        vall_ref[DEC_SEQ:, :] = vc_ref[0, 0].astype(BF16)

    lam = _lambda(lamp_ref, lam_init)
    for h in range(A_HEADS):
        sl = slice(h * LANE, (h + 1) * LANE)
        qn = _rope(_qk_norm(q_ref[:, sl].astype(F32), qg_ref[...]), cosq_ref[...], sinq_ref[...])
        o = _diff_attn_head(qn, kall_ref[:, sl], vall_ref[:, sl], lam, sg_ref[...], lam_init)
        o_ref[:, sl] = o.astype(BF16)


def _attn_lat_call(proj, cache_k, cache_v, layer, rope_cos, rope_sin, lamp, qg2, kg2, sg, lam_init):
    nq = DEC_SEQ // TQ
    r0 = TP // DEC_SEQ
    vec = pl.BlockSpec((1, LANE), lambda b, i: (0, 0))
    ctx = pl.BlockSpec((1, 1, PAST_LEN, D), lambda b, i: (b, layer, 0, 0))
    full = lambda c: pl.BlockSpec((DEC_SEQ, D), lambda b, i, c=c: (r0 + b, c))
    return pl.pallas_call(
        functools.partial(_attn_lat_kernel, lam_init=lam_init),
        out_shape=jax.ShapeDtypeStruct((TS, D), BF16),
        grid=(DEC_BATCH, nq),
        in_specs=[pl.BlockSpec((4, A_DIM), lambda b, i: (0, 0)),
                  pl.BlockSpec((TQ, D), lambda b, i: ((TP // TQ) + b * nq + i, 0)),
                  full(1), full(2), ctx, ctx,
                  pl.BlockSpec((DEC_SEQ, LANE), lambda b, i: (0, 0)),
                  pl.BlockSpec((DEC_SEQ, LANE), lambda b, i: (0, 0)),
                  pl.BlockSpec((TQ, LANE), lambda b, i: (i, 0)),
                  pl.BlockSpec((TQ, LANE), lambda b, i: (i, 0)),
                  vec, vec, vec],
        out_specs=pl.BlockSpec((TQ, D), lambda b, i: (b * nq + i, 0)),
        scratch_shapes=[pltpu.VMEM((DEC_SEQ + PAST_LEN, D), BF16),
                        pltpu.VMEM((DEC_SEQ + PAST_LEN, D), BF16)],
        compiler_params=_params(2, VMEM_LIMIT),
        name="attn_lat",
    )(lamp, proj, proj, proj, cache_k, cache_v, rope_cos, rope_sin, rope_cos, rope_sin, qg2, kg2, sg)


def _rope_tables():
    t = np.arange(DEC_SEQ)
    row, col = t // GRID_W, t % GRID_W
    lane = np.arange(LANE)
    jj = lane % A_DIM
    freq = ROPE_BASE ** (-(jj % 16).astype(np.float64) / 16.0)
    pos = np.where((jj // 32)[None, :] == 0, row[:, None], col[:, None]).astype(np.float64)
    ang = pos.astype(np.float32) * freq.astype(np.float32)[None, :]
    sign = np.where((jj % 32) < 16, -1.0, 1.0).astype(np.float32)
    return jnp.cos(jnp.asarray(ang)), jnp.sin(jnp.asarray(ang)) * jnp.asarray(sign)[None, :]


def _conv_neighbours(i):
    j = (i - TP // CONV_TILE) % (DEC_SEQ // CONV_TILE)
    lat = i >= TP // CONV_TILE
    return lat & (j > 0), lat & (j < DEC_SEQ // CONV_TILE - 1)


def _conv_kernel(a_ref, gate_ref, ap_ref, gp_ref, an_ref, gn_ref, w_ref, b_ref, g_ref, beta_ref, o_ref,
                 hp_ref, hs_ref, acc_ref):
    halo = CONV_HALO
    seq = CONV_TILE
    has_prev, has_next = _conv_neighbours(pl.program_id(0))
    glu = lambda a, g: a[...].astype(F32) * _sigmoid(g[...].astype(F32))
    hp_ref[0:halo, :] = jnp.where(has_prev, glu(ap_ref, gp_ref), 0.0)
    hp_ref[halo + seq:, :] = jnp.where(has_next, glu(an_ref, gn_ref), 0.0)
    hp_ref[halo:halo + seq, :] = glu(a_ref, gate_ref)

    srows = hs_ref.shape[1]
    for s in range(SUBLANE):
        hs_ref[s] = hp_ref[s:s + srows, :]

    for t0 in range(0, seq, CONV_ROWS):
        for c0 in range(0, D, CONV_COLS):
            acc = jnp.zeros((CONV_ROWS, CONV_COLS), F32)
            for j in range(CONV_WIDTH):
                r = halo - CONV_PAD + t0 + j
                s = r % SUBLANE
                acc = acc + (w_ref[j:j + 1, c0:c0 + CONV_COLS]
                             * hs_ref[s, r - s:r - s + CONV_ROWS, c0:c0 + CONV_COLS])
            acc_ref[t0:t0 + CONV_ROWS, c0:c0 + CONV_COLS] = acc + b_ref[:, c0:c0 + CONV_COLS]

    h = acc_ref[...]
    mu = jnp.mean(h, axis=-1, keepdims=True)
    hc = h - mu
    y = hc * lax.rsqrt(jnp.mean(hc * hc, axis=-1, keepdims=True) + EPS) * g_ref[...] + beta_ref[...]
    o_ref[...] = (y * _sigmoid(y)).astype(BF16)


def _conv_call(proj, conv_w, conv_b, cln_g, cln_b):
    vec = pl.BlockSpec((1, D), lambda i: (0, 0))
    per = CONV_TILE // CONV_HALO
    cur = lambda c: pl.BlockSpec((CONV_TILE, D), lambda i, c=c: (i, c))
    prev = lambda c: pl.BlockSpec((CONV_HALO, D), lambda i, c=c: (jnp.maximum(i * per - 1, 0), c))
    nxt = lambda c: pl.BlockSpec((CONV_HALO, D), lambda i, c=c: (jnp.minimum((i + 1) * per, T // CONV_HALO - 1), c))
    return pl.pallas_call(
        _conv_kernel,
        out_shape=jax.ShapeDtypeStruct((T, D), BF16),
        grid=(T // CONV_TILE,),
        in_specs=[cur(3), cur(4), prev(3), prev(4), nxt(3), nxt(4),
                  pl.BlockSpec((CONV_WIDTH, D), lambda i: (0, 0)),
                  vec, vec, vec],
        out_specs=pl.BlockSpec((CONV_TILE, D), lambda i: (i, 0)),
        scratch_shapes=[pltpu.VMEM((CONV_TILE + 2 * CONV_HALO, D), F32),
                        pltpu.VMEM((SUBLANE, CONV_TILE + 2 * CONV_HALO - SUBLANE, D), F32),
                        pltpu.VMEM((CONV_TILE, D), F32)],
        compiler_params=_params(1, VMEM_LIMIT),
        name="conformer_conv",
    )(proj, proj, proj, proj, proj, proj, conv_w, conv_b, cln_g, cln_b)


def _mlstm_kernel(*refs, seq, has_state, emit_state, layer, n_carried):
    if has_state:
        (m0_ref, q_ref, k_ref, v_ref, og_ref, gc_ref, gr_ref, mg_ref, c0_ref, n0_ref) = refs[:10]
        rest = refs[10:]
    else:
        (q_ref, k_ref, v_ref, og_ref, gc_ref, gr_ref, mg_ref) = refs[:7]
        rest = refs[7 + n_carried:]
    if emit_state:
        o_ref, cout_ref, nm_ref, hacc_ref, c_ref = rest
    else:
        o_ref, hacc_ref, c_ref = rest

    b_id = pl.program_id(0)
    h_id = pl.program_id(1)
    L = M_CHUNK
    nc = seq // L
    t_idx = lax.broadcasted_iota(jnp.int32, (L, L), 0)
    s_idx = lax.broadcasted_iota(jnp.int32, (L, L), 1)

    for d in range(2):
        mask = (s_idx <= t_idx) if d == 0 else (s_idx >= t_idx)
        mask_t = (t_idx <= s_idx) if d == 0 else (t_idx >= s_idx)
        last = L - 1 if d == 0 else 0
        if has_state:
            c_ref[...] = c0_ref[0, 0, d, 0]
            n = n0_ref[0, 0, d, pl.ds(h_id, 1), :]
            m = jnp.full((1, 1), m0_ref[b_id, layer, d, h_id], F32)
        else:
            n = jnp.zeros((1, M_DIM), F32)
            m = jnp.zeros((1, 1), F32)
        chunks = range(nc) if d == 0 else range(nc - 1, -1, -1)
        for ci, c in enumerate(chunks):
            rows = slice(c * L, (c + 1) * L)
            first_chunk = ci == 0
            last_chunk = ci == nc - 1
            qb = q_ref[rows, :]
            q = qb.astype(F32)
            k = k_ref[rows, :].astype(F32) * (M_DIM ** -0.5)
            kb, vb = k.astype(BF16), v_ref[rows, :]
            gi_col = gc_ref[0, rows, d:d + 1]
            gi_row = gr_ref[0, d:d + 1, rows]
            lf_col = _log_sigmoid(gc_ref[0, rows, 2 + d:3 + d])
            lf_row = _log_sigmoid(gr_ref[0, 2 + d:3 + d, rows])
            b_col = jnp.sum(jnp.where(mask, lf_row, 0.0), axis=1, keepdims=True)
            b_row = jnp.sum(jnp.where(mask_t, lf_col, 0.0), axis=0, keepdims=True)
            dmat = jnp.where(mask, b_col - b_row + gi_row, -jnp.inf)
            inter = b_col + m
            mt = jnp.maximum(inter, jnp.max(dmat, axis=1, keepdims=True))
            w = jnp.exp(dmat - mt)
            s_inter = jnp.exp(inter - mt)
            sw = _dot_nt(qb, kb) * w
            num = _dot(sw.astype(BF16), vb)
            den = jnp.sum(sw, axis=1, keepdims=True)
            if has_state or not first_chunk:
                num = num + s_inter * _dot(qb, c_ref[...].astype(BF16))
                den = den + s_inter * jnp.sum(q * n, axis=1, keepdims=True)
            hout = num / jnp.maximum(jnp.abs(den), jnp.exp(-mt))
            if d == 0:
                hacc_ref[rows, :] = hout
            else:
                hacc_ref[rows, :] = hacc_ref[rows, :] + hout
            if emit_state or not last_chunk:
                m_new = mt[last:last + 1, :]
                btot = b_col[last:last + 1, :]
                decay = jnp.exp(btot + m - m_new)
                wk_col = jnp.exp(btot - b_col + gi_col - m_new)
                kw = k * wk_col
                upd = _dot_tn(kw.astype(BF16), vb)
                nsum = jnp.sum(kw, axis=0, keepdims=True)
                if has_state or not first_chunk:
                    c_ref[...] = decay * c_ref[...] + upd
                    n = decay * n + nsum
                else:
                    c_ref[...] = upd
                    n = nsum
                m = m_new
        if emit_state:
            for s in range(cout_ref.shape[1]):
                cout_ref[0, s, d, 0] = c_ref[...]
            nm_ref[0, 0, d:d + 1, 0:M_DIM] = n
            nm_ref[0, 0, d:d + 1, M_DIM:] = jnp.broadcast_to(m, (1, LANE))

    hm = hacc_ref[...]
    ms = jnp.mean(hm * hm, axis=-1, keepdims=True)
    y = hm * lax.rsqrt(ms + EPS) * mg_ref[...]
    o_ref[...] = (y * _sigmoid(og_ref[...].astype(F32))).astype(BF16)


def _mlstm_call(proj, gates_col, gates_row, mnorm_g, seq, nb, row0, layer, states=None, carried=()):
    has_state = states is not None
    emit_state = not has_state
    aliases = {}
    col = lambda c0: pl.BlockSpec((seq, M_DIM), lambda b, h, c0=c0: (row0 + b, c0 + h))
    in_specs = [col(20), col(24), col(28), col(32),
                pl.BlockSpec((1, seq, 4), lambda b, h: (h, row0 + b, 0)),
                pl.BlockSpec((1, 4, seq), lambda b, h: (h, 0, row0 + b)),
                pl.BlockSpec((1, M_DIM), lambda b, h: (0, h))]
    args = [proj, proj, proj, proj, gates_col, gates_row, mnorm_g]
    if has_state:
        state_c, state_n, state_m = states
        in_specs = [pl.BlockSpec(memory_space=pltpu.SMEM)] + in_specs + [
            pl.BlockSpec((1, 1, 2, 1, M_DIM, M_DIM), lambda b, h: (b, layer, 0, h, 0, 0)),
            pl.BlockSpec((1, 1, 2, M_HEADS, M_DIM), lambda b, h: (b, layer, 0, 0, 0))]
        args = [state_m] + args + [state_c, state_n]
    out_shape = [jax.ShapeDtypeStruct((nb * seq, D), BF16)]
    out_specs = [pl.BlockSpec((seq, M_DIM), lambda b, h: (b, h))]
    if emit_state:
        out_shape += [jax.ShapeDtypeStruct((nb, DEPTH, 2, M_HEADS, M_DIM, M_DIM), F32),
                      jax.ShapeDtypeStruct((nb, M_HEADS, 2, M_DIM + LANE), F32)]
        if carried:
            c_spec = pl.BlockSpec((1, 1, 2, 1, M_DIM, M_DIM), lambda b, h: (b, layer, 0, h, 0, 0))
        else:
            c_spec = pl.BlockSpec((1, DEPTH, 2, 1, M_DIM, M_DIM), lambda b, h: (b, 0, 0, h, 0, 0))
        out_specs += [c_spec, pl.BlockSpec((1, 1, 2, M_DIM + LANE), lambda b, h: (b, h, 0, 0))]
        aliases = {len(args) + j: 1 + j for j in range(len(carried))}
        in_specs = in_specs + [pl.BlockSpec(memory_space=pl.ANY)] * len(carried)
        args = args + list(carried)
    return pl.pallas_call(
        functools.partial(_mlstm_kernel, seq=seq, has_state=has_state, emit_state=emit_state, layer=layer,
                          n_carried=len(carried)),
        out_shape=tuple(out_shape),
        grid=(nb, M_HEADS),
        in_specs=in_specs,
        out_specs=tuple(out_specs),
        input_output_aliases=aliases,
        scratch_shapes=[pltpu.VMEM((seq, M_DIM), F32), pltpu.VMEM((M_DIM, M_DIM), F32)],
        compiler_params=_params(2, VMEM_LIMIT),
        name="mlstm",
    )(*args)


def _mod_row_merge(i):
    per = DEC_SEQ // TM_MERGE
    return jnp.where(i < TP // TM_MERGE, 0, 1 + (i - TP // TM_MERGE) // per)


def _split3(x):
    hi = x.astype(BF16)
    lo = (x - hi.astype(F32)).astype(BF16)
    return hi, lo


def _merge_kernel(oap, oas, ob_ref, ocp, ocs, gl_ref, x_ref, mod_ref, wa_ref, wb_ref, wc_ref, wo_ref,
                  g2_ref, rwh_ref, rwl_ref, rb_ref, x1_ref, xn_ref, route_ref, cnt_ref, carry_ref):
    i = pl.program_id(0)
    is_ctx = i < TP // TM_MERGE

    @pl.when(i == 0)
    def _():
        carry_ref[...] = jnp.zeros_like(carry_ref)

    def branch(p_ref, s_ref, w_ref, c):
        o = jnp.where(is_ctx, p_ref[...], s_ref[...])
        return _sigmoid(gl_ref[:, c * D:(c + 1) * D].astype(F32)) * _dot(o, w_ref[...])

    merged = (branch(oap, oas, wa_ref, 0)
              + _sigmoid(gl_ref[:, D:2 * D].astype(F32)) * _dot(ob_ref[...], wb_ref[...])
              + branch(ocp, ocs, wc_ref, 2))
    out = _dot(merged.astype(BF16), wo_ref[...])
    x1 = x_ref[...] + mod_ref[0, 2:3, :] * out
    x1_ref[...] = x1
    ms = jnp.mean(x1 * x1, axis=-1, keepdims=True)
    xn = x1 * lax.rsqrt(ms + EPS) * g2_ref[...] * (1.0 + mod_ref[0, 4:5, :]) + mod_ref[0, 3:4, :]
    xn_ref[...] = xn

    xh, xl = _split3(xn)
    logits = _dot(xh, rwh_ref[...]) + _dot(xh, rwl_ref[...]) + _dot(xl, rwh_ref[...]) + rb_ref[...]
    lane = lax.broadcasted_iota(jnp.int32, logits.shape, 1)
    lanef = lane.astype(F32)
    cur = jnp.where(lane < N_EXPERTS, logits, -jnp.inf)
    vals, idxs = [], []
    for _ in range(TOP_K):
        mx = jnp.max(cur, axis=-1, keepdims=True)
        idx = jnp.min(jnp.where(cur == mx, lanef, float(LANE)), axis=-1, keepdims=True)
        vals.append(mx)
        idxs.append(idx)
        cur = jnp.where(lanef == idx, -jnp.inf, cur)
    exps = [jnp.exp(v - vals[0]) for v in vals]
    den = exps[0] + exps[1] + exps[2] + exps[3]

    onehot = jnp.zeros(logits.shape, F32)
    for idx in idxs:
        onehot = onehot + jnp.where(lanef == idx, 1.0, 0.0)
    r_idx = lax.broadcasted_iota(jnp.int32, (TM_MERGE, TM_MERGE), 0)
    c_idx = lax.broadcasted_iota(jnp.int32, (TM_MERGE, TM_MERGE), 1)
    tril = jnp.where(c_idx <= r_idx, 1.0, 0.0).astype(BF16)
    incl = _dot(tril, onehot.astype(BF16))
    tot = carry_ref[...] + incl
    route = jnp.zeros(logits.shape, F32)
    for k in range(TOP_K):
        rank = jnp.sum(jnp.where(lanef == idxs[k], tot, 0.0), axis=-1, keepdims=True) - 1.0
        route = jnp.where(lane == k, idxs[k], route)
        route = jnp.where(lane == TOP_K + k, exps[k] / den, route)
        route = jnp.where(lane == 2 * TOP_K + k, rank, route)
    route_ref[...] = route
    carry_ref[...] = tot[TM_MERGE - 1:TM_MERGE, :]
    cnt_ref[...] = jnp.broadcast_to(tot[TM_MERGE - 1:TM_MERGE, :], cnt_ref.shape)


def _merge_call(oa, ob, oc, proj, x, mod, wa, wb, wc, wo, g2, rwh, rwl, rb):
    npt = TP // TM_MERGE
    pblk = pl.BlockSpec((TM_MERGE, D), lambda i: (jnp.minimum(i, npt - 1), 0))
    sblk = pl.BlockSpec((TM_MERGE, D), lambda i: (jnp.maximum(i - npt, 0), 0))
    wblk = pl.BlockSpec((D, D), lambda i: (0, 0))
    rblk = pl.BlockSpec((D, LANE), lambda i: (0, 0))
    return pl.pallas_call(
        _merge_kernel,
        out_shape=(jax.ShapeDtypeStruct((T, D), F32),
                   jax.ShapeDtypeStruct((T, D), F32),
                   jax.ShapeDtypeStruct((T, LANE), F32),
                   jax.ShapeDtypeStruct((SUBLANE, LANE), F32)),
        grid=(T // TM_MERGE,),
        in_specs=[pblk, sblk, pl.BlockSpec((TM_MERGE, D), lambda i: (i, 0)), pblk, sblk,
                  pl.BlockSpec((TM_MERGE, 3 * D), lambda i: (i, 3)),
                  pl.BlockSpec((TM_MERGE, D), lambda i: (i, 0)),
                  pl.BlockSpec((1, N_MOD, D), lambda i: (_mod_row_merge(i), 0, 0)),
                  wblk, wblk, wblk, wblk,
                  pl.BlockSpec((1, D), lambda i: (0, 0)),
                  rblk, rblk,
                  pl.BlockSpec((1, LANE), lambda i: (0, 0))],
        out_specs=(pl.BlockSpec((TM_MERGE, D), lambda i: (i, 0)),
                   pl.BlockSpec((TM_MERGE, D), lambda i: (i, 0)),
                   pl.BlockSpec((TM_MERGE, LANE), lambda i: (i, 0)),
                   pl.BlockSpec((SUBLANE, LANE), lambda i: (0, 0))),
        scratch_shapes=[pltpu.VMEM((1, LANE), F32)],
        compiler_params=_params(1, VMEM_LIMIT),
        name="merge_router",
    )(oa[0], oa[1], ob, oc[0], oc[1], proj, x, mod, wa, wb, wc, wo, g2, rwh, rwl, rb)


def _row_copy(src_ref, src_row, dst_ref, dst_row, sem):
    return pltpu.make_async_copy(src_ref.at[pl.ds(src_row, 1), :], dst_ref.at[pl.ds(dst_row, 1), :], sem)


def _tile_copy(src_ref, dst_ref, sem):
    return pltpu.make_async_copy(src_ref, dst_ref, sem)


def _scatter_kernel(be_ref, nused_ref, pos_ref, x_ref, xs_ref, zeros_ref, sem):
    i = pl.program_id(0)

    @pl.when(i == 0)
    def _():
        zeros_ref[...] = jnp.zeros_like(zeros_ref)

        def partial_block(b):
            nxt = be_ref[jnp.minimum(b + 1, MOE_NB - 1)]
            return (b >= nused_ref[0] - 1) | (nxt != be_ref[b])

        def fill(b, carry):
            @pl.when(partial_block(b))
            def _():
                pltpu.make_async_copy(zeros_ref, xs_ref.at[pl.ds(b * MOE_BM, MOE_BM), :], sem).start()
            return carry

        def fill_wait(b, carry):
            @pl.when(partial_block(b))
            def _():
                pltpu.make_async_copy(zeros_ref, xs_ref.at[pl.ds(b * MOE_BM, MOE_BM), :], sem).wait()
            return carry

        lax.fori_loop(0, MOE_NB, fill, 0)
        lax.fori_loop(0, MOE_NB, fill_wait, 0)

    def start(r, carry):
        for k in range(TOP_K):
            _row_copy(x_ref, r, xs_ref, pos_ref[0, 0, r * TOP_K + k], sem).start()
        return carry

    lax.fori_loop(0, TOK_TILE, start, 0)
    for k in range(TOP_K):
        _tile_copy(x_ref, xs_ref.at[pl.ds(0, TOK_TILE), :], sem).wait()


def _scatter_call(block_e, nused, pos3, xn):
    grid_spec = pltpu.PrefetchScalarGridSpec(
        num_scalar_prefetch=2,
        grid=(T // TOK_TILE,),
        in_specs=[pl.BlockSpec((1, 1, TOK_TILE * TOP_K), lambda i, be, nu: (i, 0, 0), memory_space=pltpu.SMEM),
                  pl.BlockSpec((TOK_TILE, D), lambda i, be, nu: (i, 0))],
        out_specs=pl.BlockSpec(memory_space=pl.ANY),
        scratch_shapes=[pltpu.VMEM((MOE_BM, D), F32), pltpu.SemaphoreType.DMA],
    )
    return pl.pallas_call(
        _scatter_kernel,
        out_shape=jax.ShapeDtypeStruct((MOE_ROWS, D), F32),
        grid_spec=grid_spec,
        compiler_params=_params(1),
        name="moe_scatter",
    )(block_e, nused, pos3, xn)


def _gmm_kernel(be_ref, nused_ref, x_ref, wgu_ref, bgu_ref, wdn_ref, bdn_ref, y_ref, wgu_bf, wdn_bf):
    i = pl.program_id(0)
    prev = be_ref[jnp.maximum(i - 1, 0)]

    @pl.when((i == 0) | (be_ref[i] != prev))
    def _():
        wgu_bf[...] = wgu_ref[0, 0].astype(BF16)
        wdn_bf[...] = wdn_ref[0, 0].astype(BF16)

    @pl.when(i < nused_ref[0])
    def _():
        gu = _dot(x_ref[...].astype(BF16), wgu_bf[...]) + bgu_ref[0, 0]
        g = jnp.minimum(gu[:, :D], SWIGLU_LIMIT)
        u = jnp.clip(gu[:, D:], -SWIGLU_LIMIT, SWIGLU_LIMIT)
        hdn = (u + 1.0) * (g * _sigmoid(SWIGLU_ALPHA * g))
        y_ref[...] = _dot(hdn.astype(BF16), wdn_bf[...]) + bdn_ref[0, 0]

    @pl.when(i >= nused_ref[0])
    def _():
        y_ref[...] = jnp.zeros_like(y_ref)


def _gmm_call(block_e, nused, xs, w_gu, b_gu, w_dn, b_dn, layer):
    grid_spec = pltpu.PrefetchScalarGridSpec(
        num_scalar_prefetch=2,
        grid=(MOE_NB,),
        in_specs=[pl.BlockSpec((MOE_BM, D), lambda i, be, nu: (i, 0)),
                  pl.BlockSpec((1, 1, D, 2 * D), lambda i, be, nu: (layer, be[i], 0, 0)),
                  pl.BlockSpec((1, 1, 1, 2 * D), lambda i, be, nu: (layer, be[i], 0, 0)),
                  pl.BlockSpec((1, 1, D, D), lambda i, be, nu: (layer, be[i], 0, 0)),
                  pl.BlockSpec((1, 1, 1, D), lambda i, be, nu: (layer, be[i], 0, 0))],
        out_specs=pl.BlockSpec((MOE_BM, D), lambda i, be, nu: (i, 0)),
        scratch_shapes=[pltpu.VMEM((D, 2 * D), BF16), pltpu.VMEM((D, D), BF16)],
    )
    return pl.pallas_call(
        _gmm_kernel,
        out_shape=jax.ShapeDtypeStruct((MOE_ROWS, D), F32),
        grid_spec=grid_spec,
        compiler_params=_params(1, VMEM_LIMIT),
        name="moe_experts",
    )(block_e, nused, xs, w_gu, b_gu.reshape(DEPTH, N_EXPERTS, 1, 2 * D), w_dn,
      b_dn.reshape(DEPTH, N_EXPERTS, 1, D))


def _mod_row_tok(i):
    per = DEC_SEQ // TOK_TILE
    return jnp.where(i < TP // TOK_TILE, 0, 1 + (i - TP // TOK_TILE) // per)


def _combine_kernel(pos_ref, ys_ref, x1_ref, route_ref, mod_ref, o_ref, buf_ref, sem):
    def start(r, carry):
        for k in range(TOP_K):
            _row_copy(ys_ref, pos_ref[0, 0, r * TOP_K + k], buf_ref.at[k], r, sem).start()
        return carry

    lax.fori_loop(0, TOK_TILE, start, 0)
    for k in range(TOP_K):
        _tile_copy(ys_ref.at[pl.ds(0, TOK_TILE), :], buf_ref.at[k], sem).wait()

    acc = jnp.zeros((TOK_TILE, D), F32)
    for k in range(TOP_K):
        acc = acc + route_ref[:, TOP_K + k:TOP_K + k + 1] * buf_ref[k]
    o_ref[...] = x1_ref[...] + mod_ref[0, 5:6, :] * acc


def _combine_call(pos3, ys, x1, route, mod):
    return pl.pallas_call(
        _combine_kernel,
        out_shape=jax.ShapeDtypeStruct((T, D), F32),
        grid=(T // TOK_TILE,),
        in_specs=[pl.BlockSpec((1, 1, TOK_TILE * TOP_K), lambda i: (i, 0, 0), memory_space=pltpu.SMEM),
                  pl.BlockSpec(memory_space=pl.ANY),
                  pl.BlockSpec((TOK_TILE, D), lambda i: (i, 0)),
                  pl.BlockSpec((TOK_TILE, LANE), lambda i: (i, 0)),
                  pl.BlockSpec((1, N_MOD, D), lambda i: (_mod_row_tok(i), 0, 0))],
        out_specs=pl.BlockSpec((TOK_TILE, D), lambda i: (i, 0)),
        scratch_shapes=[pltpu.VMEM((TOP_K, TOK_TILE, D), F32), pltpu.SemaphoreType.DMA],
        compiler_params=_params(1, VMEM_LIMIT),
        name="moe_combine",
    )(pos3, ys, x1, route, mod)


def _routing_tables(route, counts):
    top_e = route[:, 0:TOP_K].astype(jnp.int32)
    rank = route[:, 2 * TOP_K:3 * TOP_K].astype(jnp.int32)
    cnt = counts[0, :N_EXPERTS].astype(jnp.int32)
    padded = (cnt + MOE_BM - 1) // MOE_BM * MOE_BM
    pad_end = jnp.cumsum(padded)
    pad_start = pad_end - padded
    pos = pad_start[top_e] + rank
    starts = jnp.arange(MOE_NB, dtype=jnp.int32) * MOE_BM
    block_e = jnp.minimum(jnp.sum((pad_end[None, :] <= starts[:, None]).astype(jnp.int32), axis=1),
                          N_EXPERTS - 1)
    nused = (pad_end[-1:] // MOE_BM).astype(jnp.int32)
    return pos.reshape(T // TOK_TILE, 1, TOK_TILE * TOP_K), block_e, nused


def kernel(x_prompt, x_sample, c, c_ctx, cache_k, cache_v, state_C, state_n, state_m, ada_w, ada_b, norm1_g, norm2_g, w_in, b_in, qn_g, kn_g, lam_q1, lam_k1, lam_q2, lam_k2, subln_g, conv_w, conv_b, cln_g, cln_b, mnorm_g, w_br_a, w_br_b, w_br_c, w_out, router_w, router_b, w_gu, b_gu, w_dn, b_dn):
    x = jnp.concatenate([x_prompt.reshape(TP, D), x_sample.reshape(TS, D)], axis=0)
    cond = jnp.concatenate([c_ctx[None, :], c, jnp.zeros((SUBLANE - 1 - DEC_BATCH, D), F32)], axis=0)
    mod_all = _ada_call(cond, ada_w, ada_b).reshape(DEPTH, SUBLANE, N_MOD, D)
    rope_cos, rope_sin = _rope_tables()
    cache_k4 = cache_k.reshape(DEC_BATCH, DEPTH, PAST_LEN, D)
    cache_v4 = cache_v.reshape(DEC_BATCH, DEPTH, PAST_LEN, D)

    kv_cache, c_state, ns, ms = (), (), [], []
    for l in range(DEPTH):
        lam_init = 0.8 - 0.6 * math.exp(-0.3 * l)
        mod = mod_all[l]
        w = w_in[l]
        w_main = jnp.concatenate([w[:, :GATE_OFF], w[:, GL_OFF:]], axis=1).astype(BF16)
        b_main = jnp.concatenate([b_in[l, :GATE_OFF], b_in[l, GL_OFF:]])[None, :]
        w_gate = jnp.pad(w[:, GATE_OFF:GL_OFF], ((0, 0), (0, LANE - 16))).astype(BF16)
        b_gate = jnp.pad(b_in[l, GATE_OFF:GL_OFF], (0, LANE - 16))[None, :]
        proj, gates = _inproj_call(x, mod, norm1_g[l][None, :], w_main, b_main, w_gate, b_gate)

        g4 = gates[:, :16].reshape(T, 2, 2, M_HEADS)
        gates_col = jnp.transpose(g4, (3, 0, 1, 2)).reshape(M_HEADS, T, 4)
        gates_row = jnp.transpose(g4, (3, 1, 2, 0)).reshape(M_HEADS, 4, T)

        lamp = jnp.stack([lam_q1[l], lam_k1[l], lam_q2[l], lam_k2[l]])
        qg2 = jnp.tile(qn_g[l], 2)[None, :]
        kg2 = jnp.tile(kn_g[l], 2)[None, :]
        sg = subln_g[l][None, :]
        oa_p, *kv_cache = _attn_ctx_call(proj, lamp, qg2, kg2, sg, lam_init, l, tuple(kv_cache))
        oa_s = _attn_lat_call(proj, cache_k4, cache_v4, l, rope_cos, rope_sin, lamp, qg2, kg2, sg, lam_init)

        cw, cb, cg, cbeta = conv_w[l], conv_b[l][None, :], cln_g[l][None, :], cln_b[l][None, :]
        ob = _conv_call(proj, cw, cb, cg, cbeta)

        mg = mnorm_g[l][None, :]
        oc_p, c_new, nm_new = _mlstm_call(proj, gates_col, gates_row, mg, SEQ, BATCH, 0, l, carried=c_state)
        c_state = (c_new,)
        (oc_s,) = _mlstm_call(proj, gates_col, gates_row, mg, DEC_SEQ, DEC_BATCH, TP // DEC_SEQ, l,
                              states=(state_C, state_n, state_m))

        rw = jnp.pad(router_w[l], ((0, 0), (0, LANE - N_EXPERTS)))
        rwh = rw.astype(BF16)
        rwl = (rw - rwh.astype(F32)).astype(BF16)
        rb = jnp.pad(router_b[l], (0, LANE - N_EXPERTS))[None, :]
        x1, xn2, route, counts = _merge_call(
            (oa_p, oa_s), ob, (oc_p, oc_s), proj, x, mod,
            w_br_a[l].astype(BF16), w_br_b[l].astype(BF16), w_br_c[l].astype(BF16), w_out[l].astype(BF16),
            norm2_g[l][None, :], rwh, rwl, rb)

        pos3, block_e, nused = _routing_tables(route, counts)
        xs = _scatter_call(block_e, nused, pos3, xn2)
        ys = _gmm_call(block_e, nused, xs, w_gu, b_gu, w_dn, b_dn, l)
        x = _combine_call(pos3, ys, x1, route, mod)

        ns.append(jnp.transpose(nm_new[..., :M_DIM], (0, 2, 1, 3)))
        ms.append(jnp.transpose(nm_new[..., M_DIM], (0, 2, 1)))

    y_p = x[:TP].reshape(BATCH, SEQ, D)
    y_s = x[TP:].reshape(DEC_BATCH, DEC_SEQ, D)
    new_k = kv_cache[0].reshape(BATCH, DEPTH, SEQ, A_HEADS, 2 * A_DIM)
    new_v = kv_cache[1].reshape(BATCH, DEPTH, SEQ, A_HEADS, A_VDIM)
    return (y_p, y_s, new_k, new_v, c_state[0], jnp.stack(ns, axis=1), jnp.stack(ms, axis=1))
```

```python
import functools
import math

import jax
import jax.numpy as jnp
import numpy as np
from jax import lax
from jax.experimental import pallas as pl
from jax.experimental.pallas import tpu as pltpu

F32 = jnp.float32
BF16 = jnp.bfloat16

D = 1024
BATCH = 32
SEQ = 256
DEPTH = 2
DEC_BATCH = 4
DEC_SEQ = 1024
PAST_LEN = 512
GRID_W = 64
EPS = 1e-6
A_HEADS = 8
A_DIM = 64
A_VDIM = 128
ROPE_BASE = 10000.0
CONV_WIDTH = 31
CONV_PAD = CONV_WIDTH // 2
M_HEADS = 4
M_DIM = 256
N_EXPERTS = 32
TOP_K = 4
SWIGLU_LIMIT = 7.0
SWIGLU_ALPHA = 1.702
N_MOD = 6

TP = BATCH * SEQ
TS = DEC_BATCH * DEC_SEQ
T = TP + TS
N_MAIN = 12 * D
GATE_OFF = 9 * D
GL_OFF = GATE_OFF + 16

LANE = 128
SUBLANE = 8
VMEM_LIMIT = 56 * 1024 * 1024

TM_IN = 1024
TN_IN = 1024
TM_MERGE = 256
TQ = 256
M_CHUNK = 256
CONV_TILE = 256
CONV_HALO = 16
CONV_ROWS = 32
CONV_COLS = 512
MOE_BM = 512
MOE_NB = T * TOP_K // MOE_BM + N_EXPERTS
MOE_ROWS = MOE_NB * MOE_BM
TOK_TILE = 256


def _sigmoid(x):
    return 1.0 / (1.0 + jnp.exp(-x))


def _log_sigmoid(x):
    return jnp.minimum(x, 0.0) - jnp.log1p(jnp.exp(-jnp.abs(x)))


def _dot(a, b):
    return jnp.dot(a, b, preferred_element_type=F32)


def _dot_nt(a, b):
    return lax.dot_general(a, b, (((1,), (1,)), ((), ())), preferred_element_type=F32)


def _dot_tn(a, b):
    return lax.dot_general(a, b, (((0,), (0,)), ((), ())), preferred_element_type=F32)


def _params(n_axes, vmem=None):
    return pltpu.CompilerParams(dimension_semantics=("arbitrary",) * n_axes,
                                vmem_limit_bytes=vmem)


def _ada_kernel(c_ref, w_ref, b_ref, o_ref):
    c = c_ref[...]
    s = (c * _sigmoid(c)).astype(BF16)
    o_ref[0] = _dot(s, w_ref[0].astype(BF16)) + b_ref[0]


def _ada_call(cond, ada_w, ada_b):
    n = N_MOD * D
    return pl.pallas_call(
        _ada_kernel,
        out_shape=jax.ShapeDtypeStruct((DEPTH, SUBLANE, n), F32),
        grid=(DEPTH, n // D),
        in_specs=[pl.BlockSpec((SUBLANE, D), lambda l, j: (0, 0)),
                  pl.BlockSpec((1, D, D), lambda l, j: (l, 0, j)),
                  pl.BlockSpec((1, 1, D), lambda l, j: (l, 0, j))],
        out_specs=pl.BlockSpec((1, SUBLANE, D), lambda l, j: (l, 0, j)),
        compiler_params=_params(2),
        name="ada_mod",
    )(cond, ada_w, ada_b.reshape(DEPTH, 1, n))


def _mod_row_in(i):
    return jnp.where(i < TP // TM_IN, 0, i - (TP // TM_IN - 1))


def _inproj_kernel(x_ref, mod_ref, g_ref, w_ref, b_ref, wg_ref, bg_ref, o_ref, og_ref, xn_ref):
    @pl.when(pl.program_id(1) == 0)
    def _():
        x = x_ref[...]
        ms = jnp.mean(x * x, axis=-1, keepdims=True)
        y = x * lax.rsqrt(ms + EPS) * g_ref[...]
        xn = (y * (1.0 + mod_ref[0, 1:2, :]) + mod_ref[0, 0:1, :]).astype(BF16)
        xn_ref[...] = xn
        og_ref[...] = _dot(xn, wg_ref[...]) + bg_ref[...]

    o_ref[...] = (_dot(xn_ref[...], w_ref[...]) + b_ref[...]).astype(BF16)


def _inproj_call(x, mod, g, w_main, b_main, w_gate, b_gate):
    return pl.pallas_call(
        _inproj_kernel,
        out_shape=(jax.ShapeDtypeStruct((T, N_MAIN), BF16),
                   jax.ShapeDtypeStruct((T, LANE), F32)),
        grid=(T // TM_IN, N_MAIN // TN_IN),
        in_specs=[pl.BlockSpec((TM_IN, D), lambda i, j: (i, 0)),
                  pl.BlockSpec((1, N_MOD, D), lambda i, j: (_mod_row_in(i), 0, 0)),
                  pl.BlockSpec((1, D), lambda i, j: (0, 0)),
                  pl.BlockSpec((D, TN_IN), lambda i, j: (0, j)),
                  pl.BlockSpec((1, TN_IN), lambda i, j: (0, j)),
                  pl.BlockSpec((D, LANE), lambda i, j: (0, 0)),
                  pl.BlockSpec((1, LANE), lambda i, j: (0, 0))],
        out_specs=(pl.BlockSpec((TM_IN, TN_IN), lambda i, j: (i, j)),
                   pl.BlockSpec((TM_IN, LANE), lambda i, j: (i, 0))),
        scratch_shapes=[pltpu.VMEM((TM_IN, D), BF16)],
        compiler_params=_params(2, VMEM_LIMIT),
        name="in_proj",
    )(x, mod, g, w_main, b_main, w_gate, b_gate)


def _qk_norm(x, g2):
    lo = lax.broadcasted_iota(jnp.int32, x.shape, 1) < A_DIM
    x2 = x * x
    s0 = jnp.sum(jnp.where(lo, x2, 0.0), axis=-1, keepdims=True)
    s1 = jnp.sum(jnp.where(lo, 0.0, x2), axis=-1, keepdims=True)
    ms = jnp.where(lo, s0, s1) * (1.0 / A_DIM)
    return x * lax.rsqrt(ms + EPS) * g2


def _rope(x, cos, sin_signed):
    first = (lax.broadcasted_iota(jnp.int32, x.shape, 1) % 32) < 16
    partner = jnp.where(first, pltpu.roll(x, LANE - 16, 1), pltpu.roll(x, 16, 1))
    return x * cos + partner * sin_signed


def _lambda(lamp_ref, lam_init):
    l1 = jnp.sum(lamp_ref[0:1, :] * lamp_ref[1:2, :], axis=-1, keepdims=True)
    l2 = jnp.sum(lamp_ref[2:3, :] * lamp_ref[3:4, :], axis=-1, keepdims=True)
    return jnp.exp(l1) - jnp.exp(l2) + lam_init


def _softmax(s):
    e = jnp.exp(s - jnp.max(s, axis=-1, keepdims=True))
    return e / jnp.sum(e, axis=-1, keepdims=True)


def _diff_attn_head(qh, kh, vh, lam, sg, lam_init):
    lo = lax.broadcasted_iota(jnp.int32, qh.shape, 1) < A_DIM
    qs = qh * (A_DIM ** -0.5)
    q0 = jnp.where(lo, qs, 0.0).astype(BF16)
    q1 = jnp.where(lo, 0.0, qs).astype(BF16)
    a0 = _softmax(_dot_nt(q0, kh))
    a1 = _softmax(_dot_nt(q1, kh))
    w = (a0 - lam * a1).astype(BF16)
    o = _dot(w, vh)
    ms = jnp.mean(o * o, axis=-1, keepdims=True)
    return o * lax.rsqrt(ms + EPS) * sg * (1.0 - lam_init)


def _attn_ctx_kernel(*refs, lam_init, n_carried):
    lamp_ref, q_ref, k_ref, v_ref, qg_ref, kg_ref, sg_ref = refs[:7]
    o_ref, kn_ref, vc_ref = refs[7 + n_carried:]
    lam = _lambda(lamp_ref, lam_init)
    n_slots = kn_ref.shape[1]
    for s in range(n_slots):
        vc_ref[0, s] = v_ref[...].astype(F32)
    for h in range(A_HEADS):
        sl = slice(h * LANE, (h + 1) * LANE)
        kn = _qk_norm(k_ref[:, sl].astype(F32), kg_ref[...])
        for s in range(n_slots):
            kn_ref[0, s, :, sl] = kn
        qn = _qk_norm(q_ref[:, sl].astype(F32), qg_ref[...])
        o = _diff_attn_head(qn, kn.astype(BF16), v_ref[:, sl], lam, sg_ref[...], lam_init)
        o_ref[:, sl] = o.astype(BF16)


def _attn_ctx_call(proj, lamp, qg2, kg2, sg, lam_init, layer, carried):
    blk = lambda c: pl.BlockSpec((SEQ, D), lambda b, c=c: (b, c))
    vec = pl.BlockSpec((1, LANE), lambda b: (0, 0))
    if carried:
        cache = pl.BlockSpec((1, 1, SEQ, D), lambda b: (b, layer, 0, 0))
    else:
        cache = pl.BlockSpec((1, DEPTH, SEQ, D), lambda b: (b, 0, 0, 0))
    cache_shape = jax.ShapeDtypeStruct((BATCH, DEPTH, SEQ, D), F32)
    n_in = 7
    return pl.pallas_call(
        functools.partial(_attn_ctx_kernel, lam_init=lam_init, n_carried=len(carried)),
        out_shape=(jax.ShapeDtypeStruct((TP, D), BF16), cache_shape, cache_shape),
        grid=(BATCH,),
        in_specs=[pl.BlockSpec((4, A_DIM), lambda b: (0, 0)), blk(0), blk(1), blk(2), vec, vec, vec]
                 + [pl.BlockSpec(memory_space=pl.ANY)] * len(carried),
        out_specs=(pl.BlockSpec((SEQ, D), lambda b: (b, 0)), cache, cache),
        input_output_aliases={n_in + j: 1 + j for j in range(len(carried))},
        compiler_params=_params(1, VMEM_LIMIT),
        name="attn_ctx",
    )(lamp, proj, proj, proj, qg2, kg2, sg, *carried)


def _attn_lat_kernel(lamp_ref, q_ref, k_ref, v_ref, kc_ref, vc_ref, cos_ref, sin_ref, cosq_ref, sinq_ref,
                     qg_ref, kg_ref, sg_ref, o_ref, kall_ref, vall_ref, *, lam_init):
    @pl.when(pl.program_id(1) == 0)
    def _():
        for h in range(A_HEADS):
            sl = slice(h * LANE, (h + 1) * LANE)
            kn = _rope(_qk_norm(k_ref[:, sl].astype(F32), kg_ref[...]), cos_ref[...], sin_ref[...])
            kall_ref[0:DEC_SEQ, sl] = kn.astype(BF16)
        vall_ref[0:DEC_SEQ, :] = v_ref[...]
        kall_ref[DEC_SEQ:, :] = kc_ref[0, 0].astype(BF16)
        vall_ref[DEC_SEQ:, :] = vc_ref[0, 0].astype(BF16)

    lam = _lambda(lamp_ref, lam_init)
    for h in range(A_HEADS):
        sl = slice(h * LANE, (h + 1) * LANE)
        qn = _rope(_qk_norm(q_ref[:, sl].astype(F32), qg_ref[...]), cosq_ref[...], sinq_ref[...])
        o = _diff_attn_head(qn, kall_ref[:, sl], vall_ref[:, sl], lam, sg_ref[...], lam_init)
        o_ref[:, sl] = o.astype(BF16)


def _attn_lat_call(proj, cache_k, cache_v, layer, rope_cos, rope_sin, lamp, qg2, kg2, sg, lam_init):
    nq = DEC_SEQ // TQ
    r0 = TP // DEC_SEQ
    vec = pl.BlockSpec((1, LANE), lambda b, i: (0, 0))
    ctx = pl.BlockSpec((1, 1, PAST_LEN, D), lambda b, i: (b, layer, 0, 0))
    full = lambda c: pl.BlockSpec((DEC_SEQ, D), lambda b, i, c=c: (r0 + b, c))
    return pl.pallas_call(
        functools.partial(_attn_lat_kernel, lam_init=lam_init),
        out_shape=jax.ShapeDtypeStruct((TS, D), BF16),
        grid=(DEC_BATCH, nq),
        in_specs=[pl.BlockSpec((4, A_DIM), lambda b, i: (0, 0)),
                  pl.BlockSpec((TQ, D), lambda b, i: ((TP // TQ) + b * nq + i, 0)),
                  full(1), full(2), ctx, ctx,
                  pl.BlockSpec((DEC_SEQ, LANE), lambda b, i: (0, 0)),
                  pl.BlockSpec((DEC_SEQ, LANE), lambda b, i: (0, 0)),
                  pl.BlockSpec((TQ, LANE), lambda b, i: (i, 0)),
                  pl.BlockSpec((TQ, LANE), lambda b, i: (i, 0)),
                  vec, vec, vec],
        out_specs=pl.BlockSpec((TQ, D), lambda b, i: (b * nq + i, 0)),
        scratch_shapes=[pltpu.VMEM((DEC_SEQ + PAST_LEN, D), BF16),
                        pltpu.VMEM((DEC_SEQ + PAST_LEN, D), BF16)],
        compiler_params=_params(2, VMEM_LIMIT),
        name="attn_lat",
    )(lamp, proj, proj, proj, cache_k, cache_v, rope_cos, rope_sin, rope_cos, rope_sin, qg2, kg2, sg)


def _rope_tables():
    t = np.arange(DEC_SEQ)
    row, col = t // GRID_W, t % GRID_W
    lane = np.arange(LANE)
    jj = lane % A_DIM
    freq = ROPE_BASE ** (-(jj % 16).astype(np.float64) / 16.0)
    pos = np.where((jj // 32)[None, :] == 0, row[:, None], col[:, None]).astype(np.float64)
    ang = pos.astype(np.float32) * freq.astype(np.float32)[None, :]
    sign = np.where((jj % 32) < 16, -1.0, 1.0).astype(np.float32)
    return jnp.cos(jnp.asarray(ang)), jnp.sin(jnp.asarray(ang)) * jnp.asarray(sign)[None, :]


def _conv_neighbours(i):
    j = (i - TP // CONV_TILE) % (DEC_SEQ // CONV_TILE)
    lat = i >= TP // CONV_TILE
    return lat & (j > 0), lat & (j < DEC_SEQ // CONV_TILE - 1)


def _conv_kernel(a_ref, gate_ref, ap_ref, gp_ref, an_ref, gn_ref, w_ref, b_ref, g_ref, beta_ref, o_ref,
                 hp_ref, hs_ref, acc_ref):
    halo = CONV_HALO
    seq = CONV_TILE
    has_prev, has_next = _conv_neighbours(pl.program_id(0))
    glu = lambda a, g: a[...].astype(F32) * _sigmoid(g[...].astype(F32))
    hp_ref[0:halo, :] = jnp.where(has_prev, glu(ap_ref, gp_ref), 0.0)
    hp_ref[halo + seq:, :] = jnp.where(has_next, glu(an_ref, gn_ref), 0.0)
    hp_ref[halo:halo + seq, :] = glu(a_ref, gate_ref)

    srows = hs_ref.shape[1]
    for s in range(SUBLANE):
        hs_ref[s] = hp_ref[s:s + srows, :]

    for t0 in range(0, seq, CONV_ROWS):
        for c0 in range(0, D, CONV_COLS):
            acc = jnp.zeros((CONV_ROWS, CONV_COLS), F32)
            for j in range(CONV_WIDTH):
                r = halo - CONV_PAD + t0 + j
                s = r % SUBLANE
                acc = acc + (w_ref[j:j + 1, c0:c0 + CONV_COLS]
                             * hs_ref[s, r - s:r - s + CONV_ROWS, c0:c0 + CONV_COLS])
            acc_ref[t0:t0 + CONV_ROWS, c0:c0 + CONV_COLS] = acc + b_ref[:, c0:c0 + CONV_COLS]

    h = acc_ref[...]
    mu = jnp.mean(h, axis=-1, keepdims=True)
    hc = h - mu
    y = hc * lax.rsqrt(jnp.mean(hc * hc, axis=-1, keepdims=True) + EPS) * g_ref[...] + beta_ref[...]
    o_ref[...] = (y * _sigmoid(y)).astype(BF16)


def _conv_call(proj, conv_w, conv_b, cln_g, cln_b):
    vec = pl.BlockSpec((1, D), lambda i: (0, 0))
    per = CONV_TILE // CONV_HALO
    cur = lambda c: pl.BlockSpec((CONV_TILE, D), lambda i, c=c: (i, c))
    prev = lambda c: pl.BlockSpec((CONV_HALO, D), lambda i, c=c: (jnp.maximum(i * per - 1, 0), c))
    nxt = lambda c: pl.BlockSpec((CONV_HALO, D), lambda i, c=c: (jnp.minimum((i + 1) * per, T // CONV_HALO - 1), c))
    return pl.pallas_call(
        _conv_kernel,
        out_shape=jax.ShapeDtypeStruct((T, D), BF16),
        grid=(T // CONV_TILE,),
        in_specs=[cur(3), cur(4), prev(3), prev(4), nxt(3), nxt(4),
                  pl.BlockSpec((CONV_WIDTH, D), lambda i: (0, 0)),
                  vec, vec, vec],
        out_specs=pl.BlockSpec((CONV_TILE, D), lambda i: (i, 0)),
        scratch_shapes=[pltpu.VMEM((CONV_TILE + 2 * CONV_HALO, D), F32),
                        pltpu.VMEM((SUBLANE, CONV_TILE + 2 * CONV_HALO - SUBLANE, D), F32),
                        pltpu.VMEM((CONV_TILE, D), F32)],
        compiler_params=_params(1, VMEM_LIMIT),
        name="conformer_conv",
    )(proj, proj, proj, proj, proj, proj, conv_w, conv_b, cln_g, cln_b)


def _mlstm_kernel(*refs, seq, has_state, emit_state, layer, n_carried):
    if has_state:
        (m0_ref, q_ref, k_ref, v_ref, og_ref, gc_ref, gr_ref, mg_ref, c0_ref, n0_ref) = refs[:10]
        rest = refs[10:]
    else:
        (q_ref, k_ref, v_ref, og_ref, gc_ref, gr_ref, mg_ref) = refs[:7]
        rest = refs[7 + n_carried:]
    if emit_state:
        o_ref, cout_ref, nm_ref, hacc_ref, c_ref = rest
    else:
        o_ref, hacc_ref, c_ref = rest

    b_id = pl.program_id(0)
    h_id = pl.program_id(1)
    L = M_CHUNK
    nc = seq // L
    t_idx = lax.broadcasted_iota(jnp.int32, (L, L), 0)
    s_idx = lax.broadcasted_iota(jnp.int32, (L, L), 1)

    for d in range(2):
        mask = (s_idx <= t_idx) if d == 0 else (s_idx >= t_idx)
        mask_t = (t_idx <= s_idx) if d == 0 else (t_idx >= s_idx)
        last = L - 1 if d == 0 else 0
        if has_state:
            c_ref[...] = c0_ref[0, 0, d, 0]
            n = n0_ref[0, 0, d, pl.ds(h_id, 1), :]
            m = jnp.full((1, 1), m0_ref[b_id, layer, d, h_id], F32)
        else:
            n = jnp.zeros((1, M_DIM), F32)
            m = jnp.zeros((1, 1), F32)
        chunks = range(nc) if d == 0 else range(nc - 1, -1, -1)
        for ci, c in enumerate(chunks):
            rows = slice(c * L, (c + 1) * L)
            first_chunk = ci == 0
            last_chunk = ci == nc - 1
            qb = q_ref[rows, :]
            q = qb.astype(F32)
            k = k_ref[rows, :].astype(F32) * (M_DIM ** -0.5)
            kb, vb = k.astype(BF16), v_ref[rows, :]
            gi_col = gc_ref[0, rows, d:d + 1]
            gi_row = gr_ref[0, d:d + 1, rows]
            lf_col = _log_sigmoid(gc_ref[0, rows, 2 + d:3 + d])
            lf_row = _log_sigmoid(gr_ref[0, 2 + d:3 + d, rows])
            b_col = jnp.sum(jnp.where(mask, lf_row, 0.0), axis=1, keepdims=True)
            b_row = jnp.sum(jnp.where(mask_t, lf_col, 0.0), axis=0, keepdims=True)
            dmat = jnp.where(mask, b_col - b_row + gi_row, -jnp.inf)
            inter = b_col + m
            mt = jnp.maximum(inter, jnp.max(dmat, axis=1, keepdims=True))
            w = jnp.exp(dmat - mt)
            s_inter = jnp.exp(inter - mt)
            sw = _dot_nt(qb, kb) * w
            num = _dot(sw.astype(BF16), vb)
            den = jnp.sum(sw, axis=1, keepdims=True)
            if has_state or not first_chunk:
                num = num + s_inter * _dot(qb, c_ref[...].astype(BF16))
                den = den + s_inter * jnp.sum(q * n, axis=1, keepdims=True)
            hout = num / jnp.maximum(jnp.abs(den), jnp.exp(-mt))
            if d == 0:
                hacc_ref[rows, :] = hout
            else:
                hacc_ref[rows, :] = hacc_ref[rows, :] + hout
            if emit_state or not last_chunk:
                m_new = mt[last:last + 1, :]
                btot = b_col[last:last + 1, :]
                decay = jnp.exp(btot + m - m_new)
                wk_col = jnp.exp(btot - b_col + gi_col - m_new)
                kw = k * wk_col
                upd = _dot_tn(kw.astype(BF16), vb)
                nsum = jnp.sum(kw, axis=0, keepdims=True)
                if has_state or not first_chunk:
                    c_ref[...] = decay * c_ref[...] + upd
                    n = decay * n + nsum
                else:
                    c_ref[...] = upd
                    n = nsum
                m = m_new
        if emit_state:
            for s in range(cout_ref.shape[1]):
                cout_ref[0, s, d, 0] = c_ref[...]
            nm_ref[0, 0, d:d + 1, 0:M_DIM] = n
            nm_ref[0, 0, d:d + 1, M_DIM:] = jnp.broadcast_to(m, (1, LANE))

    hm = hacc_ref[...]
    ms = jnp.mean(hm * hm, axis=-1, keepdims=True)
    y = hm * lax.rsqrt(ms + EPS) * mg_ref[...]
    o_ref[...] = (y * _sigmoid(og_ref[...].astype(F32))).astype(BF16)


def _mlstm_call(proj, gates_col, gates_row, mnorm_g, seq, nb, row0, layer, states=None, carried=()):
    has_state = states is not None
    emit_state = not has_state
    aliases = {}
    col = lambda c0: pl.BlockSpec((seq, M_DIM), lambda b, h, c0=c0: (row0 + b, c0 + h))
    in_specs = [col(20), col(24), col(28), col(32),
                pl.BlockSpec((1, seq, 4), lambda b, h: (h, row0 + b, 0)),
                pl.BlockSpec((1, 4, seq), lambda b, h: (h, 0, row0 + b)),
                pl.BlockSpec((1, M_DIM), lambda b, h: (0, h))]
    args = [proj, proj, proj, proj, gates_col, gates_row, mnorm_g]
    if has_state:
        state_c, state_n, state_m = states
        in_specs = [pl.BlockSpec(memory_space=pltpu.SMEM)] + in_specs + [
            pl.BlockSpec((1, 1, 2, 1, M_DIM, M_DIM), lambda b, h: (b, layer, 0, h, 0, 0)),
            pl.BlockSpec((1, 1, 2, M_HEADS, M_DIM), lambda b, h: (b, layer, 0, 0, 0))]
        args = [state_m] + args + [state_c, state_n]
    out_shape = [jax.ShapeDtypeStruct((nb * seq, D), BF16)]
    out_specs = [pl.BlockSpec((seq, M_DIM), lambda b, h: (b, h))]
    if emit_state:
        out_shape += [jax.ShapeDtypeStruct((nb, DEPTH, 2, M_HEADS, M_DIM, M_DIM), F32),
                      jax.ShapeDtypeStruct((nb, M_HEADS, 2, M_DIM + LANE), F32)]
        if carried:
            c_spec = pl.BlockSpec((1, 1, 2, 1, M_DIM, M_DIM), lambda b, h: (b, layer, 0, h, 0, 0))
        else:
            c_spec = pl.BlockSpec((1, DEPTH, 2, 1, M_DIM, M_DIM), lambda b, h: (b, 0, 0, h, 0, 0))
        out_specs += [c_spec, pl.BlockSpec((1, 1, 2, M_DIM + LANE), lambda b, h: (b, h, 0, 0))]
        aliases = {len(args) + j: 1 + j for j in range(len(carried))}
        in_specs = in_specs + [pl.BlockSpec(memory_space=pl.ANY)] * len(carried)
        args = args + list(carried)
    return pl.pallas_call(
        functools.partial(_mlstm_kernel, seq=seq, has_state=has_state, emit_state=emit_state, layer=layer,
                          n_carried=len(carried)),
        out_shape=tuple(out_shape),
        grid=(nb, M_HEADS),
        in_specs=in_specs,
        out_specs=tuple(out_specs),
        input_output_aliases=aliases,
        scratch_shapes=[pltpu.VMEM((seq, M_DIM), F32), pltpu.VMEM((M_DIM, M_DIM), F32)],
        compiler_params=_params(2, VMEM_LIMIT),
        name="mlstm",
    )(*args)


def _mod_row_merge(i):
    per = DEC_SEQ // TM_MERGE
    return jnp.where(i < TP // TM_MERGE, 0, 1 + (i - TP // TM_MERGE) // per)


def _split3(x):
    hi = x.astype(BF16)
    lo = (x - hi.astype(F32)).astype(BF16)
    return hi, lo


def _merge_kernel(oap, oas, ob_ref, ocp, ocs, gl_ref, x_ref, mod_ref, wa_ref, wb_ref, wc_ref, wo_ref,
                  g2_ref, rwh_ref, rwl_ref, rb_ref, x1_ref, xn_ref, route_ref, cnt_ref, carry_ref):
    i = pl.program_id(0)
    is_ctx = i < TP // TM_MERGE

    @pl.when(i == 0)
    def _():
        carry_ref[...] = jnp.zeros_like(carry_ref)

    def branch(p_ref, s_ref, w_ref, c):
        o = jnp.where(is_ctx, p_ref[...], s_ref[...])
        return _sigmoid(gl_ref[:, c * D:(c + 1) * D].astype(F32)) * _dot(o, w_ref[...])

    merged = (branch(oap, oas, wa_ref, 0)
              + _sigmoid(gl_ref[:, D:2 * D].astype(F32)) * _dot(ob_ref[...], wb_ref[...])
              + branch(ocp, ocs, wc_ref, 2))
    out = _dot(merged.astype(BF16), wo_ref[...])
    x1 = x_ref[...] + mod_ref[0, 2:3, :] * out
    x1_ref[...] = x1
    ms = jnp.mean(x1 * x1, axis=-1, keepdims=True)
    xn = x1 * lax.rsqrt(ms + EPS) * g2_ref[...] * (1.0 + mod_ref[0, 4:5, :]) + mod_ref[0, 3:4, :]
    xn_ref[...] = xn

    xh, xl = _split3(xn)
    logits = _dot(xh, rwh_ref[...]) + _dot(xh, rwl_ref[...]) + _dot(xl, rwh_ref[...]) + rb_ref[...]
    lane = lax.broadcasted_iota(jnp.int32, logits.shape, 1)
    lanef = lane.astype(F32)
    cur = jnp.where(lane < N_EXPERTS, logits, -jnp.inf)
    vals, idxs = [], []
    for _ in range(TOP_K):
        mx = jnp.max(cur, axis=-1, keepdims=True)
        idx = jnp.min(jnp.where(cur == mx, lanef, float(LANE)), axis=-1, keepdims=True)
        vals.append(mx)
        idxs.append(idx)
        cur = jnp.where(lanef == idx, -jnp.inf, cur)
    exps = [jnp.exp(v - vals[0]) for v in vals]
    den = exps[0] + exps[1] + exps[2] + exps[3]

    onehot = jnp.zeros(logits.shape, F32)
    for idx in idxs:
        onehot = onehot + jnp.where(lanef == idx, 1.0, 0.0)
    r_idx = lax.broadcasted_iota(jnp.int32, (TM_MERGE, TM_MERGE), 0)
    c_idx = lax.broadcasted_iota(jnp.int32, (TM_MERGE, TM_MERGE), 1)
    tril = jnp.where(c_idx <= r_idx, 1.0, 0.0).astype(BF16)
    incl = _dot(tril, onehot.astype(BF16))
    tot = carry_ref[...] + incl
    route = jnp.zeros(logits.shape, F32)
    for k in range(TOP_K):
        rank = jnp.sum(jnp.where(lanef == idxs[k], tot, 0.0), axis=-1, keepdims=True) - 1.0
        route = jnp.where(lane == k, idxs[k], route)
        route = jnp.where(lane == TOP_K + k, exps[k] / den, route)
        route = jnp.where(lane == 2 * TOP_K + k, rank, route)
    route_ref[...] = route
    carry_ref[...] = tot[TM_MERGE - 1:TM_MERGE, :]
    cnt_ref[...] = jnp.broadcast_to(tot[TM_MERGE - 1:TM_MERGE, :], cnt_ref.shape)


def _merge_call(oa, ob, oc, proj, x, mod, wa, wb, wc, wo, g2, rwh, rwl, rb):
    npt = TP // TM_MERGE
    pblk = pl.BlockSpec((TM_MERGE, D), lambda i: (jnp.minimum(i, npt - 1), 0))
    sblk = pl.BlockSpec((TM_MERGE, D), lambda i: (jnp.maximum(i - npt, 0), 0))
    wblk = pl.BlockSpec((D, D), lambda i: (0, 0))
    rblk = pl.BlockSpec((D, LANE), lambda i: (0, 0))
    return pl.pallas_call(
        _merge_kernel,
        out_shape=(jax.ShapeDtypeStruct((T, D), F32),
                   jax.ShapeDtypeStruct((T, D), F32),
                   jax.ShapeDtypeStruct((T, LANE), F32),
                   jax.ShapeDtypeStruct((SUBLANE, LANE), F32)),
        grid=(T // TM_MERGE,),
        in_specs=[pblk, sblk, pl.BlockSpec((TM_MERGE, D), lambda i: (i, 0)), pblk, sblk,
                  pl.BlockSpec((TM_MERGE, 3 * D), lambda i: (i, 3)),
                  pl.BlockSpec((TM_MERGE, D), lambda i: (i, 0)),
                  pl.BlockSpec((1, N_MOD, D), lambda i: (_mod_row_merge(i), 0, 0)),
                  wblk, wblk, wblk, wblk,
                  pl.BlockSpec((1, D), lambda i: (0, 0)),
                  rblk, rblk,
                  pl.BlockSpec((1, LANE), lambda i: (0, 0))],
        out_specs=(pl.BlockSpec((TM_MERGE, D), lambda i: (i, 0)),
                   pl.BlockSpec((TM_MERGE, D), lambda i: (i, 0)),
                   pl.BlockSpec((TM_MERGE, LANE), lambda i: (i, 0)),
                   pl.BlockSpec((SUBLANE, LANE), lambda i: (0, 0))),
        scratch_shapes=[pltpu.VMEM((1, LANE), F32)],
        compiler_params=_params(1, VMEM_LIMIT),
        name="merge_router",
    )(oa[0], oa[1], ob, oc[0], oc[1], proj, x, mod, wa, wb, wc, wo, g2, rwh, rwl, rb)


def _row_copy(src_ref, src_row, dst_ref, dst_row, sem):
    return pltpu.make_async_copy(src_ref.at[pl.ds(src_row, 1), :], dst_ref.at[pl.ds(dst_row, 1), :], sem)


def _tile_copy(src_ref, dst_ref, sem):
    return pltpu.make_async_copy(src_ref, dst_ref, sem)


def _scatter_kernel(be_ref, nused_ref, pos_ref, x_ref, xs_ref, zeros_ref, sem):
    i = pl.program_id(0)

    @pl.when(i == 0)
    def _():
        zeros_ref[...] = jnp.zeros_like(zeros_ref)

        def partial_block(b):
            nxt = be_ref[jnp.minimum(b + 1, MOE_NB - 1)]
            return (b >= nused_ref[0] - 1) | (nxt != be_ref[b])

        def fill(b, carry):
            @pl.when(partial_block(b))
            def _():
                pltpu.make_async_copy(zeros_ref, xs_ref.at[pl.ds(b * MOE_BM, MOE_BM), :], sem).start()
            return carry

        def fill_wait(b, carry):
            @pl.when(partial_block(b))
            def _():
                pltpu.make_async_copy(zeros_ref, xs_ref.at[pl.ds(b * MOE_BM, MOE_BM), :], sem).wait()
            return carry

        lax.fori_loop(0, MOE_NB, fill, 0)
        lax.fori_loop(0, MOE_NB, fill_wait, 0)

    def start(r, carry):
        for k in range(TOP_K):
            _row_copy(x_ref, r, xs_ref, pos_ref[0, 0, r * TOP_K + k], sem).start()
        return carry

    lax.fori_loop(0, TOK_TILE, start, 0)
    for k in range(TOP_K):
        _tile_copy(x_ref, xs_ref.at[pl.ds(0, TOK_TILE), :], sem).wait()


def _scatter_call(block_e, nused, pos3, xn):
    grid_spec = pltpu.PrefetchScalarGridSpec(
        num_scalar_prefetch=2,
        grid=(T // TOK_TILE,),
        in_specs=[pl.BlockSpec((1, 1, TOK_TILE * TOP_K), lambda i, be, nu: (i, 0, 0), memory_space=pltpu.SMEM),
                  pl.BlockSpec((TOK_TILE, D), lambda i, be, nu: (i, 0))],
        out_specs=pl.BlockSpec(memory_space=pl.ANY),
        scratch_shapes=[pltpu.VMEM((MOE_BM, D), F32), pltpu.SemaphoreType.DMA],
    )
    return pl.pallas_call(
        _scatter_kernel,
        out_shape=jax.ShapeDtypeStruct((MOE_ROWS, D), F32),
        grid_spec=grid_spec,
        compiler_params=_params(1),
        name="moe_scatter",
    )(block_e, nused, pos3, xn)


def _gmm_kernel(be_ref, valid_ref, x_ref, wgu_ref, bgu_ref, wdn_ref, bdn_ref, y_ref, wgu_bf, wdn_bf):
    i = pl.program_id(0)
    prev = be_ref[jnp.maximum(i - 1, 0)]
    valid = valid_ref[i]

    @pl.when((i == 0) | (be_ref[i] != prev))
    def _():
        wgu_bf[...] = wgu_ref[0, 0].astype(BF16)
        wdn_bf[...] = wdn_ref[0, 0].astype(BF16)

    def expert_mlp(rows):
        gu = _dot(x_ref[0:rows, :].astype(BF16), wgu_bf[...]) + bgu_ref[0, 0]
        g = jnp.minimum(gu[:, :D], SWIGLU_LIMIT)
        u = jnp.clip(gu[:, D:], -SWIGLU_LIMIT, SWIGLU_LIMIT)
        hdn = (u + 1.0) * (g * _sigmoid(SWIGLU_ALPHA * g))
        y_ref[0:rows, :] = _dot(hdn.astype(BF16), wdn_bf[...]) + bdn_ref[0, 0]
        if rows < MOE_BM:
            y_ref[rows:, :] = jnp.zeros((MOE_BM - rows, D), F32)

    @pl.when(valid > MOE_BM // 2)
    def _():
        expert_mlp(MOE_BM)

    @pl.when((valid > 0) & (valid <= MOE_BM // 2))
    def _():
        expert_mlp(MOE_BM // 2)

    @pl.when(valid == 0)
    def _():
        y_ref[...] = jnp.zeros_like(y_ref)


def _gmm_call(block_e, valid, xs, w_gu, b_gu, w_dn, b_dn, layer):
    grid_spec = pltpu.PrefetchScalarGridSpec(
        num_scalar_prefetch=2,
        grid=(MOE_NB,),
        in_specs=[pl.BlockSpec((MOE_BM, D), lambda i, be, nu: (i, 0)),
                  pl.BlockSpec((1, 1, D, 2 * D), lambda i, be, nu: (layer, be[i], 0, 0)),
                  pl.BlockSpec((1, 1, 1, 2 * D), lambda i, be, nu: (layer, be[i], 0, 0)),
                  pl.BlockSpec((1, 1, D, D), lambda i, be, nu: (layer, be[i], 0, 0)),
                  pl.BlockSpec((1, 1, 1, D), lambda i, be, nu: (layer, be[i], 0, 0))],
        out_specs=pl.BlockSpec((MOE_BM, D), lambda i, be, nu: (i, 0)),
        scratch_shapes=[pltpu.VMEM((D, 2 * D), BF16), pltpu.VMEM((D, D), BF16)],
    )
    return pl.pallas_call(
        _gmm_kernel,
        out_shape=jax.ShapeDtypeStruct((MOE_ROWS, D), F32),
        grid_spec=grid_spec,
        compiler_params=_params(1, VMEM_LIMIT),
        name="moe_experts",
    )(block_e, valid, xs, w_gu, b_gu.reshape(DEPTH, N_EXPERTS, 1, 2 * D), w_dn,
      b_dn.reshape(DEPTH, N_EXPERTS, 1, D))


def _mod_row_tok(i):
    per = DEC_SEQ // TOK_TILE
    return jnp.where(i < TP // TOK_TILE, 0, 1 + (i - TP // TOK_TILE) // per)


def _combine_kernel(pos_ref, ys_ref, x1_ref, route_ref, mod_ref, o_ref, buf_ref, sem):
    def start(r, carry):
        for k in range(TOP_K):
            _row_copy(ys_ref, pos_ref[0, 0, r * TOP_K + k], buf_ref.at[k], r, sem).start()
        return carry

    lax.fori_loop(0, TOK_TILE, start, 0)
    for k in range(TOP_K):
        _tile_copy(ys_ref.at[pl.ds(0, TOK_TILE), :], buf_ref.at[k], sem).wait()

    acc = jnp.zeros((TOK_TILE, D), F32)
    for k in range(TOP_K):
        acc = acc + route_ref[:, TOP_K + k:TOP_K + k + 1] * buf_ref[k]
    o_ref[...] = x1_ref[...] + mod_ref[0, 5:6, :] * acc


def _combine_call(pos3, ys, x1, route, mod):
    return pl.pallas_call(
        _combine_kernel,
        out_shape=jax.ShapeDtypeStruct((T, D), F32),
        grid=(T // TOK_TILE,),
        in_specs=[pl.BlockSpec((1, 1, TOK_TILE * TOP_K), lambda i: (i, 0, 0), memory_space=pltpu.SMEM),
                  pl.BlockSpec(memory_space=pl.ANY),
                  pl.BlockSpec((TOK_TILE, D), lambda i: (i, 0)),
                  pl.BlockSpec((TOK_TILE, LANE), lambda i: (i, 0)),
                  pl.BlockSpec((1, N_MOD, D), lambda i: (_mod_row_tok(i), 0, 0))],
        out_specs=pl.BlockSpec((TOK_TILE, D), lambda i: (i, 0)),
        scratch_shapes=[pltpu.VMEM((TOP_K, TOK_TILE, D), F32), pltpu.SemaphoreType.DMA],
        compiler_params=_params(1, VMEM_LIMIT),
        name="moe_combine",
    )(pos3, ys, x1, route, mod)


def _pos_kernel(route_ref, start_ref, o_ref):
    r = route_ref[...]
    lane = lax.broadcasted_iota(jnp.int32, r.shape, 1)
    lanef = lane.astype(F32)
    out = jnp.zeros(r.shape, jnp.int32)
    for k in range(TOP_K):
        first = jnp.sum(jnp.where(lanef == r[:, k:k + 1], start_ref[...], 0.0), axis=-1, keepdims=True)
        pos = first + r[:, 2 * TOP_K + k:2 * TOP_K + k + 1]
        out = jnp.where(lane == k, pos.astype(jnp.int32), out)
    o_ref[...] = out


def _pos_call(route, pad_start):
    return pl.pallas_call(
        _pos_kernel,
        out_shape=jax.ShapeDtypeStruct((T, LANE), jnp.int32),
        grid=(T // TM_IN,),
        in_specs=[pl.BlockSpec((TM_IN, LANE), lambda i: (i, 0)),
                  pl.BlockSpec((1, LANE), lambda i: (0, 0))],
        out_specs=pl.BlockSpec((TM_IN, LANE), lambda i: (i, 0)),
        compiler_params=_params(1),
        name="moe_rows",
    )(route, pad_start)


def _routing_tables(route, counts):
    cnt = counts[0, :N_EXPERTS].astype(jnp.int32)
    padded = (cnt + MOE_BM - 1) // MOE_BM * MOE_BM
    pad_end = jnp.cumsum(padded)
    pad_start = pad_end - padded
    starts = jnp.arange(MOE_NB, dtype=jnp.int32) * MOE_BM
    block_e = jnp.minimum(jnp.sum((pad_end[None, :] <= starts[:, None]).astype(jnp.int32), axis=1),
                          N_EXPERTS - 1)
    valid = jnp.clip((pad_start + cnt)[block_e] - starts, 0, MOE_BM)
    nused = (pad_end[-1:] // MOE_BM).astype(jnp.int32)
    start_row = jnp.pad(pad_start.astype(F32), (0, LANE - N_EXPERTS))[None, :]
    pos = _pos_call(route, start_row)[:, :TOP_K]
    return pos.reshape(T // TOK_TILE, 1, TOK_TILE * TOP_K), block_e, nused, valid


def kernel(x_prompt, x_sample, c, c_ctx, cache_k, cache_v, state_C, state_n, state_m, ada_w, ada_b, norm1_g, norm2_g, w_in, b_in, qn_g, kn_g, lam_q1, lam_k1, lam_q2, lam_k2, subln_g, conv_w, conv_b, cln_g, cln_b, mnorm_g, w_br_a, w_br_b, w_br_c, w_out, router_w, router_b, w_gu, b_gu, w_dn, b_dn):
    x = jnp.concatenate([x_prompt.reshape(TP, D), x_sample.reshape(TS, D)], axis=0)
    cond = jnp.concatenate([c_ctx[None, :], c, jnp.zeros((SUBLANE - 1 - DEC_BATCH, D), F32)], axis=0)
    mod_all = _ada_call(cond, ada_w, ada_b).reshape(DEPTH, SUBLANE, N_MOD, D)
    rope_cos, rope_sin = _rope_tables()
    cache_k4 = cache_k.reshape(DEC_BATCH, DEPTH, PAST_LEN, D)
    cache_v4 = cache_v.reshape(DEC_BATCH, DEPTH, PAST_LEN, D)

    kv_cache, c_state, ns, ms = (), (), [], []
    for l in range(DEPTH):
        lam_init = 0.8 - 0.6 * math.exp(-0.3 * l)
        mod = mod_all[l]
        w = w_in[l]
        w_main = jnp.concatenate([w[:, :GATE_OFF], w[:, GL_OFF:]], axis=1).astype(BF16)
        b_main = jnp.concatenate([b_in[l, :GATE_OFF], b_in[l, GL_OFF:]])[None, :]
        w_gate = jnp.pad(w[:, GATE_OFF:GL_OFF], ((0, 0), (0, LANE - 16))).astype(BF16)
        b_gate = jnp.pad(b_in[l, GATE_OFF:GL_OFF], (0, LANE - 16))[None, :]
        proj, gates = _inproj_call(x, mod, norm1_g[l][None, :], w_main, b_main, w_gate, b_gate)

        g4 = gates[:, :16].reshape(T, 2, 2, M_HEADS)
        gates_col = jnp.transpose(g4, (3, 0, 1, 2)).reshape(M_HEADS, T, 4)
        gates_row = jnp.transpose(g4, (3, 1, 2, 0)).reshape(M_HEADS, 4, T)

        lamp = jnp.stack([lam_q1[l], lam_k1[l], lam_q2[l], lam_k2[l]])
        qg2 = jnp.tile(qn_g[l], 2)[None, :]
        kg2 = jnp.tile(kn_g[l], 2)[None, :]
        sg = subln_g[l][None, :]
        oa_p, *kv_cache = _attn_ctx_call(proj, lamp, qg2, kg2, sg, lam_init, l, tuple(kv_cache))
        oa_s = _attn_lat_call(proj, cache_k4, cache_v4, l, rope_cos, rope_sin, lamp, qg2, kg2, sg, lam_init)

        cw, cb, cg, cbeta = conv_w[l], conv_b[l][None, :], cln_g[l][None, :], cln_b[l][None, :]
        ob = _conv_call(proj, cw, cb, cg, cbeta)

        mg = mnorm_g[l][None, :]
        oc_p, c_new, nm_new = _mlstm_call(proj, gates_col, gates_row, mg, SEQ, BATCH, 0, l, carried=c_state)
        c_state = (c_new,)
        (oc_s,) = _mlstm_call(proj, gates_col, gates_row, mg, DEC_SEQ, DEC_BATCH, TP // DEC_SEQ, l,
                              states=(state_C, state_n, state_m))

        rw = jnp.pad(router_w[l], ((0, 0), (0, LANE - N_EXPERTS)))
        rwh = rw.astype(BF16)
        rwl = (rw - rwh.astype(F32)).astype(BF16)
        rb = jnp.pad(router_b[l], (0, LANE - N_EXPERTS))[None, :]
        x1, xn2, route, counts = _merge_call(
            (oa_p, oa_s), ob, (oc_p, oc_s), proj, x, mod,
            w_br_a[l].astype(BF16), w_br_b[l].astype(BF16), w_br_c[l].astype(BF16), w_out[l].astype(BF16),
            norm2_g[l][None, :], rwh, rwl, rb)

        pos3, block_e, nused, valid = _routing_tables(route, counts)
        xs = _scatter_call(block_e, nused, pos3, xn2)
        ys = _gmm_call(block_e, valid, xs, w_gu, b_gu, w_dn, b_dn, l)
        x = _combine_call(pos3, ys, x1, route, mod)

        ns.append(jnp.transpose(nm_new[..., :M_DIM], (0, 2, 1, 3)))
        ms.append(jnp.transpose(nm_new[..., M_DIM], (0, 2, 1)))

    y_p = x[:TP].reshape(BATCH, SEQ, D)
    y_s = x[TP:].reshape(DEC_BATCH, DEC_SEQ, D)
    new_k = kv_cache[0].reshape(BATCH, DEPTH, SEQ, A_HEADS, 2 * A_DIM)
    new_v = kv_cache[1].reshape(BATCH, DEPTH, SEQ, A_HEADS, A_VDIM)
    return (y_p, y_s, new_k, new_v, c_state[0], jnp.stack(ns, axis=1), jnp.stack(ms, axis=1))
```

```python
import functools
import math

import jax
import jax.numpy as jnp
import numpy as np
from jax import lax
from jax.experimental import pallas as pl
from jax.experimental.pallas import tpu as pltpu

F32 = jnp.float32
BF16 = jnp.bfloat16

D = 1024
BATCH = 32
SEQ = 256
DEPTH = 2
DEC_BATCH = 4
DEC_SEQ = 1024
PAST_LEN = 512
GRID_W = 64
EPS = 1e-6
A_HEADS = 8
A_DIM = 64
A_VDIM = 128
ROPE_BASE = 10000.0
CONV_WIDTH = 31
CONV_PAD = CONV_WIDTH // 2
M_HEADS = 4
M_DIM = 256
N_EXPERTS = 32
TOP_K = 4
SWIGLU_LIMIT = 7.0
SWIGLU_ALPHA = 1.702
N_MOD = 6

TP = BATCH * SEQ
TS = DEC_BATCH * DEC_SEQ
T = TP + TS
N_MAIN = 12 * D
GATE_OFF = 9 * D
GL_OFF = GATE_OFF + 16

LANE = 128
SUBLANE = 8
VMEM_LIMIT = 56 * 1024 * 1024

TM_IN = 1024
TN_IN = 1024
TM_MERGE = 512
TQ = 256
M_CHUNK = 256
CONV_TILE = 256
CONV_HALO = 16
CONV_ROWS = 32
CONV_COLS = 512
MOE_BM = 512
MOE_NB = T * TOP_K // MOE_BM + N_EXPERTS
MOE_ROWS = MOE_NB * MOE_BM
TOK_TILE = 256


def _sigmoid(x):
    return 1.0 / (1.0 + jnp.exp(-x))


def _log_sigmoid(x):
    return jnp.minimum(x, 0.0) - jnp.log1p(jnp.exp(-jnp.abs(x)))


def _dot(a, b):
    return jnp.dot(a, b, preferred_element_type=F32)


def _dot_nt(a, b):
    return lax.dot_general(a, b, (((1,), (1,)), ((), ())), preferred_element_type=F32)


def _dot_tn(a, b):
    return lax.dot_general(a, b, (((0,), (0,)), ((), ())), preferred_element_type=F32)


def _params(n_axes, vmem=None):
    return pltpu.CompilerParams(dimension_semantics=("arbitrary",) * n_axes,
                                vmem_limit_bytes=vmem)


def _ada_kernel(c_ref, w_ref, b_ref, o_ref):
    c = c_ref[...]
    s = (c * _sigmoid(c)).astype(BF16)
    o_ref[0] = _dot(s, w_ref[0].astype(BF16)) + b_ref[0]


def _ada_call(cond, ada_w, ada_b):
    n = N_MOD * D
    return pl.pallas_call(
        _ada_kernel,
        out_shape=jax.ShapeDtypeStruct((DEPTH, SUBLANE, n), F32),
        grid=(DEPTH, n // D),
        in_specs=[pl.BlockSpec((SUBLANE, D), lambda l, j: (0, 0)),
                  pl.BlockSpec((1, D, D), lambda l, j: (l, 0, j)),
                  pl.BlockSpec((1, 1, D), lambda l, j: (l, 0, j))],
        out_specs=pl.BlockSpec((1, SUBLANE, D), lambda l, j: (l, 0, j)),
        compiler_params=_params(2),
        name="ada_mod",
    )(cond, ada_w, ada_b.reshape(DEPTH, 1, n))


def _mod_row_in(i):
    return jnp.where(i < TP // TM_IN, 0, i - (TP // TM_IN - 1))


def _inproj_kernel(x_ref, mod_ref, g_ref, w_ref, b_ref, wg_ref, bg_ref, o_ref, og_ref, xn_ref):
    @pl.when(pl.program_id(1) == 0)
    def _():
        x = x_ref[...]
        ms = jnp.mean(x * x, axis=-1, keepdims=True)
        y = x * lax.rsqrt(ms + EPS) * g_ref[...]
        xn = (y * (1.0 + mod_ref[0, 1:2, :]) + mod_ref[0, 0:1, :]).astype(BF16)
        xn_ref[...] = xn
        og_ref[...] = _dot(xn, wg_ref[...]) + bg_ref[...]

    o_ref[...] = (_dot(xn_ref[...], w_ref[...]) + b_ref[...]).astype(BF16)


def _inproj_call(x, mod, g, w_main, b_main, w_gate, b_gate):
    return pl.pallas_call(
        _inproj_kernel,
        out_shape=(jax.ShapeDtypeStruct((T, N_MAIN), BF16),
                   jax.ShapeDtypeStruct((T, LANE), F32)),
        grid=(T // TM_IN, N_MAIN // TN_IN),
        in_specs=[pl.BlockSpec((TM_IN, D), lambda i, j: (i, 0)),
                  pl.BlockSpec((1, N_MOD, D), lambda i, j: (_mod_row_in(i), 0, 0)),
                  pl.BlockSpec((1, D), lambda i, j: (0, 0)),
                  pl.BlockSpec((D, TN_IN), lambda i, j: (0, j)),
                  pl.BlockSpec((1, TN_IN), lambda i, j: (0, j)),
                  pl.BlockSpec((D, LANE), lambda i, j: (0, 0)),
                  pl.BlockSpec((1, LANE), lambda i, j: (0, 0))],
        out_specs=(pl.BlockSpec((TM_IN, TN_IN), lambda i, j: (i, j)),
                   pl.BlockSpec((TM_IN, LANE), lambda i, j: (i, 0))),
        scratch_shapes=[pltpu.VMEM((TM_IN, D), BF16)],
        compiler_params=_params(2, VMEM_LIMIT),
        name="in_proj",
    )(x, mod, g, w_main, b_main, w_gate, b_gate)


def _qk_norm(x, g2):
    lo = lax.broadcasted_iota(jnp.int32, x.shape, 1) < A_DIM
    x2 = x * x
    s0 = jnp.sum(jnp.where(lo, x2, 0.0), axis=-1, keepdims=True)
    s1 = jnp.sum(jnp.where(lo, 0.0, x2), axis=-1, keepdims=True)
    ms = jnp.where(lo, s0, s1) * (1.0 / A_DIM)
    return x * lax.rsqrt(ms + EPS) * g2


def _rope(x, cos, sin_signed):
    first = (lax.broadcasted_iota(jnp.int32, x.shape, 1) % 32) < 16
    partner = jnp.where(first, pltpu.roll(x, LANE - 16, 1), pltpu.roll(x, 16, 1))
    return x * cos + partner * sin_signed


def _lambda(lamp_ref, lam_init):
    l1 = jnp.sum(lamp_ref[0:1, :] * lamp_ref[1:2, :], axis=-1, keepdims=True)
    l2 = jnp.sum(lamp_ref[2:3, :] * lamp_ref[3:4, :], axis=-1, keepdims=True)
    return jnp.exp(l1) - jnp.exp(l2) + lam_init


def _softmax(s):
    e = jnp.exp(s - jnp.max(s, axis=-1, keepdims=True))
    return e / jnp.sum(e, axis=-1, keepdims=True)


def _diff_attn_head(qh, kh, vh, lam, sg, lam_init):
    lo = lax.broadcasted_iota(jnp.int32, qh.shape, 1) < A_DIM
    qs = qh * (A_DIM ** -0.5)
    q0 = jnp.where(lo, qs, 0.0).astype(BF16)
    q1 = jnp.where(lo, 0.0, qs).astype(BF16)
    a0 = _softmax(_dot_nt(q0, kh))
    a1 = _softmax(_dot_nt(q1, kh))
    w = (a0 - lam * a1).astype(BF16)
    o = _dot(w, vh)
    ms = jnp.mean(o * o, axis=-1, keepdims=True)
    return o * lax.rsqrt(ms + EPS) * sg * (1.0 - lam_init)


def _attn_ctx_kernel(*refs, lam_init, n_carried):
    lamp_ref, q_ref, k_ref, v_ref, qg_ref, kg_ref, sg_ref = refs[:7]
    o_ref, kn_ref, vc_ref = refs[7 + n_carried:]
    lam = _lambda(lamp_ref, lam_init)
    n_slots = kn_ref.shape[1]
    for s in range(n_slots):
        vc_ref[0, s] = v_ref[...].astype(F32)
    for h in range(A_HEADS):
        sl = slice(h * LANE, (h + 1) * LANE)
        kn = _qk_norm(k_ref[:, sl].astype(F32), kg_ref[...])
        for s in range(n_slots):
            kn_ref[0, s, :, sl] = kn
        qn = _qk_norm(q_ref[:, sl].astype(F32), qg_ref[...])
        o = _diff_attn_head(qn, kn.astype(BF16), v_ref[:, sl], lam, sg_ref[...], lam_init)
        o_ref[:, sl] = o.astype(BF16)


def _attn_ctx_call(proj, lamp, qg2, kg2, sg, lam_init, layer, carried):
    blk = lambda c: pl.BlockSpec((SEQ, D), lambda b, c=c: (b, c))
    vec = pl.BlockSpec((1, LANE), lambda b: (0, 0))
    if carried:
        cache = pl.BlockSpec((1, 1, SEQ, D), lambda b: (b, layer, 0, 0))
    else:
        cache = pl.BlockSpec((1, DEPTH, SEQ, D), lambda b: (b, 0, 0, 0))
    cache_shape = jax.ShapeDtypeStruct((BATCH, DEPTH, SEQ, D), F32)
    n_in = 7
    return pl.pallas_call(
        functools.partial(_attn_ctx_kernel, lam_init=lam_init, n_carried=len(carried)),
        out_shape=(jax.ShapeDtypeStruct((TP, D), BF16), cache_shape, cache_shape),
        grid=(BATCH,),
        in_specs=[pl.BlockSpec((4, A_DIM), lambda b: (0, 0)), blk(0), blk(1), blk(2), vec, vec, vec]
                 + [pl.BlockSpec(memory_space=pl.ANY)] * len(carried),
        out_specs=(pl.BlockSpec((SEQ, D), lambda b: (b, 0)), cache, cache),
        input_output_aliases={n_in + j: 1 + j for j in range(len(carried))},
        compiler_params=_params(1, VMEM_LIMIT),
        name="attn_ctx",
    )(lamp, proj, proj, proj, qg2, kg2, sg, *carried)


def _attn_lat_kernel(lamp_ref, q_ref, k_ref, v_ref, kc_ref, vc_ref, cos_ref, sin_ref, cosq_ref, sinq_ref,
                     qg_ref, kg_ref, sg_ref, o_ref, kall_ref, vall_ref, *, lam_init):
    @pl.when(pl.program_id(1) == 0)
    def _():
        for h in range(A_HEADS):
            sl = slice(h * LANE, (h + 1) * LANE)
            kn = _rope(_qk_norm(k_ref[:, sl].astype(F32), kg_ref[...]), cos_ref[...], sin_ref[...])
            kall_ref[0:DEC_SEQ, sl] = kn.astype(BF16)
        vall_ref[0:DEC_SEQ, :] = v_ref[...]
        kall_ref[DEC_SEQ:, :] = kc_ref[0, 0].astype(BF16)
        vall_ref[DEC_SEQ:, :] = vc_ref[0, 0].astype(BF16)

    lam = _lambda(lamp_ref, lam_init)
    for h in range(A_HEADS):
        sl = slice(h * LANE, (h + 1) * LANE)
        qn = _rope(_qk_norm(q_ref[:, sl].astype(F32), qg_ref[...]), cosq_ref[...], sinq_ref[...])
        o = _diff_attn_head(qn, kall_ref[:, sl], vall_ref[:, sl], lam, sg_ref[...], lam_init)
        o_ref[:, sl] = o.astype(BF16)


def _attn_lat_call(proj, cache_k, cache_v, layer, rope_cos, rope_sin, lamp, qg2, kg2, sg, lam_init):
    nq = DEC_SEQ // TQ
    r0 = TP // DEC_SEQ
    vec = pl.BlockSpec((1, LANE), lambda b, i: (0, 0))
    ctx = pl.BlockSpec((1, 1, PAST_LEN, D), lambda b, i: (b, layer, 0, 0))
    full = lambda c: pl.BlockSpec((DEC_SEQ, D), lambda b, i, c=c: (r0 + b, c))
    return pl.pallas_call(
        functools.partial(_attn_lat_kernel, lam_init=lam_init),
        out_shape=jax.ShapeDtypeStruct((TS, D), BF16),
        grid=(DEC_BATCH, nq),
        in_specs=[pl.BlockSpec((4, A_DIM), lambda b, i: (0, 0)),
                  pl.BlockSpec((TQ, D), lambda b, i: ((TP // TQ) + b * nq + i, 0)),
                  full(1), full(2), ctx, ctx,
                  pl.BlockSpec((DEC_SEQ, LANE), lambda b, i: (0, 0)),
                  pl.BlockSpec((DEC_SEQ, LANE), lambda b, i: (0, 0)),
                  pl.BlockSpec((TQ, LANE), lambda b, i: (i, 0)),
                  pl.BlockSpec((TQ, LANE), lambda b, i: (i, 0)),
                  vec, vec, vec],
        out_specs=pl.BlockSpec((TQ, D), lambda b, i: (b * nq + i, 0)),
        scratch_shapes=[pltpu.VMEM((DEC_SEQ + PAST_LEN, D), BF16),
                        pltpu.VMEM((DEC_SEQ + PAST_LEN, D), BF16)],
        compiler_params=_params(2, VMEM_LIMIT),
        name="attn_lat",
    )(lamp, proj, proj, proj, cache_k, cache_v, rope_cos, rope_sin, rope_cos, rope_sin, qg2, kg2, sg)


def _rope_tables():
    t = np.arange(DEC_SEQ)
    row, col = t // GRID_W, t % GRID_W
    lane = np.arange(LANE)
    jj = lane % A_DIM
    freq = ROPE_BASE ** (-(jj % 16).astype(np.float64) / 16.0)
    pos = np.where((jj // 32)[None, :] == 0, row[:, None], col[:, None]).astype(np.float64)
    ang = pos.astype(np.float32) * freq.astype(np.float32)[None, :]
    sign = np.where((jj % 32) < 16, -1.0, 1.0).astype(np.float32)
    return jnp.cos(jnp.asarray(ang)), jnp.sin(jnp.asarray(ang)) * jnp.asarray(sign)[None, :]


def _conv_neighbours(i):
    j = (i - TP // CONV_TILE) % (DEC_SEQ // CONV_TILE)
    lat = i >= TP // CONV_TILE
    return lat & (j > 0), lat & (j < DEC_SEQ // CONV_TILE - 1)


def _conv_kernel(a_ref, gate_ref, ap_ref, gp_ref, an_ref, gn_ref, w_ref, b_ref, g_ref, beta_ref, o_ref,
                 hp_ref, hs_ref, acc_ref):
    halo = CONV_HALO
    seq = CONV_TILE
    has_prev, has_next = _conv_neighbours(pl.program_id(0))
    glu = lambda a, g: a[...].astype(F32) * _sigmoid(g[...].astype(F32))
    hp_ref[0:halo, :] = jnp.where(has_prev, glu(ap_ref, gp_ref), 0.0)
    hp_ref[halo + seq:, :] = jnp.where(has_next, glu(an_ref, gn_ref), 0.0)
    hp_ref[halo:halo + seq, :] = glu(a_ref, gate_ref)

    srows = hs_ref.shape[1]
    for s in range(SUBLANE):
        hs_ref[s] = hp_ref[s:s + srows, :]

    for t0 in range(0, seq, CONV_ROWS):
        for c0 in range(0, D, CONV_COLS):
            acc = jnp.zeros((CONV_ROWS, CONV_COLS), F32)
            for j in range(CONV_WIDTH):
                r = halo - CONV_PAD + t0 + j
                s = r % SUBLANE
                acc = acc + (w_ref[j:j + 1, c0:c0 + CONV_COLS]
                             * hs_ref[s, r - s:r - s + CONV_ROWS, c0:c0 + CONV_COLS])
            acc_ref[t0:t0 + CONV_ROWS, c0:c0 + CONV_COLS] = acc + b_ref[:, c0:c0 + CONV_COLS]

    h = acc_ref[...]
    mu = jnp.mean(h, axis=-1, keepdims=True)
    hc = h - mu
    y = hc * lax.rsqrt(jnp.mean(hc * hc, axis=-1, keepdims=True) + EPS) * g_ref[...] + beta_ref[...]
    o_ref[...] = (y * _sigmoid(y)).astype(BF16)


def _conv_call(proj, conv_w, conv_b, cln_g, cln_b):
    vec = pl.BlockSpec((1, D), lambda i: (0, 0))
    per = CONV_TILE // CONV_HALO
    cur = lambda c: pl.BlockSpec((CONV_TILE, D), lambda i, c=c: (i, c))
    prev = lambda c: pl.BlockSpec((CONV_HALO, D), lambda i, c=c: (jnp.maximum(i * per - 1, 0), c))
    nxt = lambda c: pl.BlockSpec((CONV_HALO, D), lambda i, c=c: (jnp.minimum((i + 1) * per, T // CONV_HALO - 1), c))
    return pl.pallas_call(
        _conv_kernel,
        out_shape=jax.ShapeDtypeStruct((T, D), BF16),
        grid=(T // CONV_TILE,),
        in_specs=[cur(3), cur(4), prev(3), prev(4), nxt(3), nxt(4),
                  pl.BlockSpec((CONV_WIDTH, D), lambda i: (0, 0)),
                  vec, vec, vec],
        out_specs=pl.BlockSpec((CONV_TILE, D), lambda i: (i, 0)),
        scratch_shapes=[pltpu.VMEM((CONV_TILE + 2 * CONV_HALO, D), F32),
                        pltpu.VMEM((SUBLANE, CONV_TILE + 2 * CONV_HALO - SUBLANE, D), F32),
                        pltpu.VMEM((CONV_TILE, D), F32)],
        compiler_params=_params(1, VMEM_LIMIT),
        name="conformer_conv",
    )(proj, proj, proj, proj, proj, proj, conv_w, conv_b, cln_g, cln_b)


def _mlstm_kernel(*refs, seq, has_state, emit_state, layer, n_carried):
    if has_state:
        (m0_ref, q_ref, k_ref, v_ref, og_ref, gc_ref, gr_ref, mg_ref, c0_ref, n0_ref) = refs[:10]
        rest = refs[10:]
    else:
        (q_ref, k_ref, v_ref, og_ref, gc_ref, gr_ref, mg_ref) = refs[:7]
        rest = refs[7 + n_carried:]
    if emit_state:
        o_ref, cout_ref, nm_ref, hacc_ref, c_ref = rest
    else:
        o_ref, hacc_ref, c_ref = rest

    b_id = pl.program_id(0)
    h_id = pl.program_id(1)
    L = M_CHUNK
    nc = seq // L
    t_idx = lax.broadcasted_iota(jnp.int32, (L, L), 0)
    s_idx = lax.broadcasted_iota(jnp.int32, (L, L), 1)

    for d in range(2):
        mask = (s_idx <= t_idx) if d == 0 else (s_idx >= t_idx)
        mask_t = (t_idx <= s_idx) if d == 0 else (t_idx >= s_idx)
        last = L - 1 if d == 0 else 0
        if has_state:
            c_ref[...] = c0_ref[0, 0, d, 0]
            n = n0_ref[0, 0, d, pl.ds(h_id, 1), :]
            m = jnp.full((1, 1), m0_ref[b_id, layer, d, h_id], F32)
        else:
            n = jnp.zeros((1, M_DIM), F32)
            m = jnp.zeros((1, 1), F32)
        chunks = range(nc) if d == 0 else range(nc - 1, -1, -1)
        for ci, c in enumerate(chunks):
            rows = slice(c * L, (c + 1) * L)
            first_chunk = ci == 0
            last_chunk = ci == nc - 1
            qb = q_ref[rows, :]
            q = qb.astype(F32)
            k = k_ref[rows, :].astype(F32) * (M_DIM ** -0.5)
            kb, vb = k.astype(BF16), v_ref[rows, :]
            gi_col = gc_ref[0, rows, d:d + 1]
            gi_row = gr_ref[0, d:d + 1, rows]
            lf_col = _log_sigmoid(gc_ref[0, rows, 2 + d:3 + d])
            lf_row = _log_sigmoid(gr_ref[0, 2 + d:3 + d, rows])
            b_col = jnp.sum(jnp.where(mask, lf_row, 0.0), axis=1, keepdims=True)
            b_row = jnp.sum(jnp.where(mask_t, lf_col, 0.0), axis=0, keepdims=True)
            dmat = jnp.where(mask, b_col - b_row + gi_row, -jnp.inf)
            inter = b_col + m
            mt = jnp.maximum(inter, jnp.max(dmat, axis=1, keepdims=True))
            w = jnp.exp(dmat - mt)
            s_inter = jnp.exp(inter - mt)
            sw = _dot_nt(qb, kb) * w
            num = _dot(sw.astype(BF16), vb)
            den = jnp.sum(sw, axis=1, keepdims=True)
            if has_state or not first_chunk:
                num = num + s_inter * _dot(qb, c_ref[...].astype(BF16))
                den = den + s_inter * jnp.sum(q * n, axis=1, keepdims=True)
            hout = num / jnp.maximum(jnp.abs(den), jnp.exp(-mt))
            if d == 0:
                hacc_ref[rows, :] = hout
            else:
                hacc_ref[rows, :] = hacc_ref[rows, :] + hout
            if emit_state or not last_chunk:
                m_new = mt[last:last + 1, :]
                btot = b_col[last:last + 1, :]
                decay = jnp.exp(btot + m - m_new)
                wk_col = jnp.exp(btot - b_col + gi_col - m_new)
                kw = k * wk_col
                upd = _dot_tn(kw.astype(BF16), vb)
                nsum = jnp.sum(kw, axis=0, keepdims=True)
                if has_state or not first_chunk:
                    c_ref[...] = decay * c_ref[...] + upd
                    n = decay * n + nsum
                else:
                    c_ref[...] = upd
                    n = nsum
                m = m_new
        if emit_state:
            for s in range(cout_ref.shape[1]):
                cout_ref[0, s, d, 0] = c_ref[...]
            nm_ref[0, 0, d:d + 1, 0:M_DIM] = n
            nm_ref[0, 0, d:d + 1, M_DIM:] = jnp.broadcast_to(m, (1, LANE))

    hm = hacc_ref[...]
    ms = jnp.mean(hm * hm, axis=-1, keepdims=True)
    y = hm * lax.rsqrt(ms + EPS) * mg_ref[...]
    o_ref[...] = (y * _sigmoid(og_ref[...].astype(F32))).astype(BF16)


def _mlstm_call(proj, gates_col, gates_row, mnorm_g, seq, nb, row0, layer, states=None, carried=()):
    has_state = states is not None
    emit_state = not has_state
    aliases = {}
    col = lambda c0: pl.BlockSpec((seq, M_DIM), lambda b, h, c0=c0: (row0 + b, c0 + h))
    in_specs = [col(20), col(24), col(28), col(32),
                pl.BlockSpec((1, seq, 4), lambda b, h: (h, row0 + b, 0)),
                pl.BlockSpec((1, 4, seq), lambda b, h: (h, 0, row0 + b)),
                pl.BlockSpec((1, M_DIM), lambda b, h: (0, h))]
    args = [proj, proj, proj, proj, gates_col, gates_row, mnorm_g]
    if has_state:
        state_c, state_n, state_m = states
        in_specs = [pl.BlockSpec(memory_space=pltpu.SMEM)] + in_specs + [
            pl.BlockSpec((1, 1, 2, 1, M_DIM, M_DIM), lambda b, h: (b, layer, 0, h, 0, 0)),
            pl.BlockSpec((1, 1, 2, M_HEADS, M_DIM), lambda b, h: (b, layer, 0, 0, 0))]
        args = [state_m] + args + [state_c, state_n]
    out_shape = [jax.ShapeDtypeStruct((nb * seq, D), BF16)]
    out_specs = [pl.BlockSpec((seq, M_DIM), lambda b, h: (b, h))]
    if emit_state:
        out_shape += [jax.ShapeDtypeStruct((nb, DEPTH, 2, M_HEADS, M_DIM, M_DIM), F32),
                      jax.ShapeDtypeStruct((nb, M_HEADS, 2, M_DIM + LANE), F32)]
        if carried:
            c_spec = pl.BlockSpec((1, 1, 2, 1, M_DIM, M_DIM), lambda b, h: (b, layer, 0, h, 0, 0))
        else:
            c_spec = pl.BlockSpec((1, DEPTH, 2, 1, M_DIM, M_DIM), lambda b, h: (b, 0, 0, h, 0, 0))
        out_specs += [c_spec, pl.BlockSpec((1, 1, 2, M_DIM + LANE), lambda b, h: (b, h, 0, 0))]
        aliases = {len(args) + j: 1 + j for j in range(len(carried))}
        in_specs = in_specs + [pl.BlockSpec(memory_space=pl.ANY)] * len(carried)
        args = args + list(carried)
    return pl.pallas_call(
        functools.partial(_mlstm_kernel, seq=seq, has_state=has_state, emit_state=emit_state, layer=layer,
                          n_carried=len(carried)),
        out_shape=tuple(out_shape),
        grid=(nb, M_HEADS),
        in_specs=in_specs,
        out_specs=tuple(out_specs),
        input_output_aliases=aliases,
        scratch_shapes=[pltpu.VMEM((seq, M_DIM), F32), pltpu.VMEM((M_DIM, M_DIM), F32)],
        compiler_params=_params(2, VMEM_LIMIT),
        name="mlstm",
    )(*args)


def _mod_row_merge(i):
    per = DEC_SEQ // TM_MERGE
    return jnp.where(i < TP // TM_MERGE, 0, 1 + (i - TP // TM_MERGE) // per)


def _split3(x):
    hi = x.astype(BF16)
    lo = (x - hi.astype(F32)).astype(BF16)
    return hi, lo


def _merge_kernel(oap, oas, ob_ref, ocp, ocs, gl_ref, x_ref, mod_ref, wa_ref, wb_ref, wc_ref, wo_ref,
                  g2_ref, rwh_ref, rwl_ref, rb_ref, x1_ref, xn_ref, route_ref, cnt_ref, carry_ref):
    i = pl.program_id(0)
    is_ctx = i < TP // TM_MERGE

    @pl.when(i == 0)
    def _():
        carry_ref[...] = jnp.zeros_like(carry_ref)

    def branch(p_ref, s_ref, w_ref, c):
        o = jnp.where(is_ctx, p_ref[...], s_ref[...])
        return _sigmoid(gl_ref[:, c * D:(c + 1) * D].astype(F32)) * _dot(o, w_ref[...])

    merged = (branch(oap, oas, wa_ref, 0)
              + _sigmoid(gl_ref[:, D:2 * D].astype(F32)) * _dot(ob_ref[...], wb_ref[...])
              + branch(ocp, ocs, wc_ref, 2))
    out = _dot(merged.astype(BF16), wo_ref[...])
    x1 = x_ref[...] + mod_ref[0, 2:3, :] * out
    x1_ref[...] = x1
    ms = jnp.mean(x1 * x1, axis=-1, keepdims=True)
    xn = x1 * lax.rsqrt(ms + EPS) * g2_ref[...] * (1.0 + mod_ref[0, 4:5, :]) + mod_ref[0, 3:4, :]
    xn_ref[...] = xn

    xh, xl = _split3(xn)
    logits = _dot(xh, rwh_ref[...]) + _dot(xh, rwl_ref[...]) + _dot(xl, rwh_ref[...]) + rb_ref[...]
    lane = lax.broadcasted_iota(jnp.int32, logits.shape, 1)
    lanef = lane.astype(F32)
    cur = jnp.where(lane < N_EXPERTS, logits, -jnp.inf)
    vals, idxs = [], []
    for _ in range(TOP_K):
        mx = jnp.max(cur, axis=-1, keepdims=True)
        idx = jnp.min(jnp.where(cur == mx, lanef, float(LANE)), axis=-1, keepdims=True)
        vals.append(mx)
        idxs.append(idx)
        cur = jnp.where(lanef == idx, -jnp.inf, cur)
    exps = [jnp.exp(v - vals[0]) for v in vals]
    den = exps[0] + exps[1] + exps[2] + exps[3]

    onehot = jnp.zeros(logits.shape, F32)
    for idx in idxs:
        onehot = onehot + jnp.where(lanef == idx, 1.0, 0.0)
    r_idx = lax.broadcasted_iota(jnp.int32, (TM_MERGE, TM_MERGE), 0)
    c_idx = lax.broadcasted_iota(jnp.int32, (TM_MERGE, TM_MERGE), 1)
    tril = jnp.where(c_idx <= r_idx, 1.0, 0.0).astype(BF16)
    incl = _dot(tril, onehot.astype(BF16))
    tot = carry_ref[...] + incl
    route = jnp.zeros(logits.shape, F32)
    for k in range(TOP_K):
        rank = jnp.sum(jnp.where(lanef == idxs[k], tot, 0.0), axis=-1, keepdims=True) - 1.0
        route = jnp.where(lane == k, idxs[k], route)
        route = jnp.where(lane == TOP_K + k, exps[k] / den, route)
        route = jnp.where(lane == 2 * TOP_K + k, rank, route)
    route_ref[...] = route
    carry_ref[...] = tot[TM_MERGE - 1:TM_MERGE, :]
    cnt_ref[...] = jnp.broadcast_to(tot[TM_MERGE - 1:TM_MERGE, :], cnt_ref.shape)


def _merge_call(oa, ob, oc, proj, x, mod, wa, wb, wc, wo, g2, rwh, rwl, rb):
    npt = TP // TM_MERGE
    pblk = pl.BlockSpec((TM_MERGE, D), lambda i: (jnp.minimum(i, npt - 1), 0))
    sblk = pl.BlockSpec((TM_MERGE, D), lambda i: (jnp.maximum(i - npt, 0), 0))
    wblk = pl.BlockSpec((D, D), lambda i: (0, 0))
    rblk = pl.BlockSpec((D, LANE), lambda i: (0, 0))
    return pl.pallas_call(
        _merge_kernel,
        out_shape=(jax.ShapeDtypeStruct((T, D), F32),
                   jax.ShapeDtypeStruct((T, D), F32),
                   jax.ShapeDtypeStruct((T, LANE), F32),
                   jax.ShapeDtypeStruct((SUBLANE, LANE), F32)),
        grid=(T // TM_MERGE,),
        in_specs=[pblk, sblk, pl.BlockSpec((TM_MERGE, D), lambda i: (i, 0)), pblk, sblk,
                  pl.BlockSpec((TM_MERGE, 3 * D), lambda i: (i, 3)),
                  pl.BlockSpec((TM_MERGE, D), lambda i: (i, 0)),
                  pl.BlockSpec((1, N_MOD, D), lambda i: (_mod_row_merge(i), 0, 0)),
                  wblk, wblk, wblk, wblk,
                  pl.BlockSpec((1, D), lambda i: (0, 0)),
                  rblk, rblk,
                  pl.BlockSpec((1, LANE), lambda i: (0, 0))],
        out_specs=(pl.BlockSpec((TM_MERGE, D), lambda i: (i, 0)),
                   pl.BlockSpec((TM_MERGE, D), lambda i: (i, 0)),
                   pl.BlockSpec((TM_MERGE, LANE), lambda i: (i, 0)),
                   pl.BlockSpec((SUBLANE, LANE), lambda i: (0, 0))),
        scratch_shapes=[pltpu.VMEM((1, LANE), F32)],
        compiler_params=_params(1, VMEM_LIMIT),
        name="merge_router",
    )(oa[0], oa[1], ob, oc[0], oc[1], proj, x, mod, wa, wb, wc, wo, g2, rwh, rwl, rb)


def _row_copy(src_ref, src_row, dst_ref, dst_row, sem):
    return pltpu.make_async_copy(src_ref.at[pl.ds(src_row, 1), :], dst_ref.at[pl.ds(dst_row, 1), :], sem)


def _tile_copy(src_ref, dst_ref, sem):
    return pltpu.make_async_copy(src_ref, dst_ref, sem)


def _scatter_kernel(be_ref, nused_ref, pos_ref, x_ref, xs_ref, zeros_ref, sem):
    i = pl.program_id(0)
    last = pl.num_programs(0) - 1

    @pl.when(i == 0)
    def _():
        zeros_ref[...] = jnp.zeros_like(zeros_ref)

        def partial_block(b):
            nxt = be_ref[jnp.minimum(b + 1, MOE_NB - 1)]
            return (b >= nused_ref[0] - 1) | (nxt != be_ref[b])

        def fill(b, carry):
            @pl.when(partial_block(b))
            def _():
                pltpu.make_async_copy(zeros_ref, xs_ref.at[pl.ds(b * MOE_BM, MOE_BM), :], sem).start()
            return carry

        def fill_wait(b, carry):
            @pl.when(partial_block(b))
            def _():
                pltpu.make_async_copy(zeros_ref, xs_ref.at[pl.ds(b * MOE_BM, MOE_BM), :], sem).wait()
            return carry

        lax.fori_loop(0, MOE_NB, fill, 0)
        lax.fori_loop(0, MOE_NB, fill_wait, 0)

    def start(r, carry):
        for k in range(TOP_K):
            _row_copy(x_ref, i * TOK_TILE + r, xs_ref, pos_ref[0, 0, r * TOP_K + k], sem).start()
        return carry

    lax.fori_loop(0, TOK_TILE, start, 0)

    def wait_tile():
        for k in range(TOP_K):
            _tile_copy(x_ref.at[pl.ds(0, TOK_TILE), :], xs_ref.at[pl.ds(0, TOK_TILE), :], sem).wait()

    @pl.when(i > 0)
    def _():
        wait_tile()

    @pl.when(i == last)
    def _():
        wait_tile()


def _scatter_call(block_e, nused, pos3, xn):
    grid_spec = pltpu.PrefetchScalarGridSpec(
        num_scalar_prefetch=2,
        grid=(T // TOK_TILE,),
        in_specs=[pl.BlockSpec((1, 1, TOK_TILE * TOP_K), lambda i, be, nu: (i, 0, 0), memory_space=pltpu.SMEM),
                  pl.BlockSpec(memory_space=pl.ANY)],
        out_specs=pl.BlockSpec(memory_space=pl.ANY),
        scratch_shapes=[pltpu.VMEM((MOE_BM, D), F32), pltpu.SemaphoreType.DMA],
    )
    return pl.pallas_call(
        _scatter_kernel,
        out_shape=jax.ShapeDtypeStruct((MOE_ROWS, D), F32),
        grid_spec=grid_spec,
        compiler_params=_params(1),
        name="moe_scatter",
    )(block_e, nused, pos3, xn)


def _gmm_kernel(be_ref, valid_ref, x_ref, wgu_ref, bgu_ref, wdn_ref, bdn_ref, y_ref, wgu_bf, wdn_bf):
    i = pl.program_id(0)
    prev = be_ref[jnp.maximum(i - 1, 0)]
    valid = valid_ref[i]

    @pl.when((i == 0) | (be_ref[i] != prev))
    def _():
        wgu_bf[...] = wgu_ref[0, 0].astype(BF16)
        wdn_bf[...] = wdn_ref[0, 0].astype(BF16)

    def expert_mlp(rows):
        gu = _dot(x_ref[0:rows, :].astype(BF16), wgu_bf[...]) + bgu_ref[0, 0]
        g = jnp.minimum(gu[:, :D], SWIGLU_LIMIT)
        u = jnp.clip(gu[:, D:], -SWIGLU_LIMIT, SWIGLU_LIMIT)
        hdn = (u + 1.0) * (g * _sigmoid(SWIGLU_ALPHA * g))
        y_ref[0:rows, :] = _dot(hdn.astype(BF16), wdn_bf[...]) + bdn_ref[0, 0]
        if rows < MOE_BM:
            y_ref[rows:, :] = jnp.zeros((MOE_BM - rows, D), F32)

    @pl.when(valid > MOE_BM // 2)
    def _():
        expert_mlp(MOE_BM)

    @pl.when((valid > 0) & (valid <= MOE_BM // 2))
    def _():
        expert_mlp(MOE_BM // 2)

    @pl.when(valid == 0)
    def _():
        y_ref[...] = jnp.zeros_like(y_ref)


def _gmm_call(block_e, valid, xs, w_gu, b_gu, w_dn, b_dn, layer):
    grid_spec = pltpu.PrefetchScalarGridSpec(
        num_scalar_prefetch=2,
        grid=(MOE_NB,),
        in_specs=[pl.BlockSpec((MOE_BM, D), lambda i, be, nu: (i, 0)),
                  pl.BlockSpec((1, 1, D, 2 * D), lambda i, be, nu: (layer, be[i], 0, 0)),
                  pl.BlockSpec((1, 1, 1, 2 * D), lambda i, be, nu: (layer, be[i], 0, 0)),
                  pl.BlockSpec((1, 1, D, D), lambda i, be, nu: (layer, be[i], 0, 0)),
                  pl.BlockSpec((1, 1, 1, D), lambda i, be, nu: (layer, be[i], 0, 0))],
        out_specs=pl.BlockSpec((MOE_BM, D), lambda i, be, nu: (i, 0)),
        scratch_shapes=[pltpu.VMEM((D, 2 * D), BF16), pltpu.VMEM((D, D), BF16)],
    )
    return pl.pallas_call(
        _gmm_kernel,
        out_shape=jax.ShapeDtypeStruct((MOE_ROWS, D), F32),
        grid_spec=grid_spec,
        compiler_params=_params(1, VMEM_LIMIT),
        name="moe_experts",
    )(block_e, valid, xs, w_gu, b_gu.reshape(DEPTH, N_EXPERTS, 1, 2 * D), w_dn,
      b_dn.reshape(DEPTH, N_EXPERTS, 1, D))


def _mod_row_tok(i):
    per = DEC_SEQ // TOK_TILE
    return jnp.where(i < TP // TOK_TILE, 0, 1 + (i - TP // TOK_TILE) // per)


def _combine_kernel(pos_ref, pos_next_ref, ys_ref, x1_ref, route_ref, mod_ref, o_ref, buf_ref, sems):
    i = pl.program_id(0)
    last = pl.num_programs(0) - 1
    slot = i % 2

    def gather(p_ref, s):
        def start(r, carry):
            for k in range(TOP_K):
                _row_copy(ys_ref, p_ref[0, 0, r * TOP_K + k], buf_ref.at[s, k], r, sems.at[s]).start()
            return carry

        lax.fori_loop(0, TOK_TILE, start, 0)

    @pl.when(i == 0)
    def _():
        gather(pos_ref, slot)

    @pl.when(i < last)
    def _():
        gather(pos_next_ref, 1 - slot)

    for k in range(TOP_K):
        _tile_copy(ys_ref.at[pl.ds(0, TOK_TILE), :], buf_ref.at[slot, k], sems.at[slot]).wait()

    acc = jnp.zeros((TOK_TILE, D), F32)
    for k in range(TOP_K):
        acc = acc + route_ref[:, TOP_K + k:TOP_K + k + 1] * buf_ref[slot, k]
    o_ref[...] = x1_ref[...] + mod_ref[0, 5:6, :] * acc


def _combine_call(pos3, ys, x1, route, mod):
    n_tiles = T // TOK_TILE
    pos_spec = lambda f: pl.BlockSpec((1, 1, TOK_TILE * TOP_K), lambda i: (f(i), 0, 0), memory_space=pltpu.SMEM)
    return pl.pallas_call(
        _combine_kernel,
        out_shape=jax.ShapeDtypeStruct((T, D), F32),
        grid=(n_tiles,),
        in_specs=[pos_spec(lambda i: i), pos_spec(lambda i: jnp.minimum(i + 1, n_tiles - 1)),
                  pl.BlockSpec(memory_space=pl.ANY),
                  pl.BlockSpec((TOK_TILE, D), lambda i: (i, 0)),
                  pl.BlockSpec((TOK_TILE, LANE), lambda i: (i, 0)),
                  pl.BlockSpec((1, N_MOD, D), lambda i: (_mod_row_tok(i), 0, 0))],
        out_specs=pl.BlockSpec((TOK_TILE, D), lambda i: (i, 0)),
        scratch_shapes=[pltpu.VMEM((2, TOP_K, TOK_TILE, D), F32), pltpu.SemaphoreType.DMA((2,))],
        compiler_params=_params(1, VMEM_LIMIT),
        name="moe_combine",
    )(pos3, pos3, ys, x1, route, mod)


def _pos_kernel(route_ref, start_ref, o_ref):
    r = route_ref[...]
    lane = lax.broadcasted_iota(jnp.int32, r.shape, 1)
    lanef = lane.astype(F32)
    out = jnp.zeros(r.shape, jnp.int32)
    for k in range(TOP_K):
        first = jnp.sum(jnp.where(lanef == r[:, k:k + 1], start_ref[...], 0.0), axis=-1, keepdims=True)
        pos = first + r[:, 2 * TOP_K + k:2 * TOP_K + k + 1]
        out = jnp.where(lane == k, pos.astype(jnp.int32), out)
    o_ref[...] = out


def _pos_call(route, pad_start):
    return pl.pallas_call(
        _pos_kernel,
        out_shape=jax.ShapeDtypeStruct((T, LANE), jnp.int32),
        grid=(T // TM_IN,),
        in_specs=[pl.BlockSpec((TM_IN, LANE), lambda i: (i, 0)),
                  pl.BlockSpec((1, LANE), lambda i: (0, 0))],
        out_specs=pl.BlockSpec((TM_IN, LANE), lambda i: (i, 0)),
        compiler_params=_params(1),
        name="moe_rows",
    )(route, pad_start)


def _routing_tables(route, counts):
    cnt = counts[0, :N_EXPERTS].astype(jnp.int32)
    padded = (cnt + MOE_BM - 1) // MOE_BM * MOE_BM
    pad_end = jnp.cumsum(padded)
    pad_start = pad_end - padded
    starts = jnp.arange(MOE_NB, dtype=jnp.int32) * MOE_BM
    block_e = jnp.minimum(jnp.sum((pad_end[None, :] <= starts[:, None]).astype(jnp.int32), axis=1),
                          N_EXPERTS - 1)
    valid = jnp.clip((pad_start + cnt)[block_e] - starts, 0, MOE_BM)
    nused = (pad_end[-1:] // MOE_BM).astype(jnp.int32)
    start_row = jnp.pad(pad_start.astype(F32), (0, LANE - N_EXPERTS))[None, :]
    pos = _pos_call(route, start_row)[:, :TOP_K]
    return pos.reshape(T // TOK_TILE, 1, TOK_TILE * TOP_K), block_e, nused, valid


def kernel(x_prompt, x_sample, c, c_ctx, cache_k, cache_v, state_C, state_n, state_m, ada_w, ada_b, norm1_g, norm2_g, w_in, b_in, qn_g, kn_g, lam_q1, lam_k1, lam_q2, lam_k2, subln_g, conv_w, conv_b, cln_g, cln_b, mnorm_g, w_br_a, w_br_b, w_br_c, w_out, router_w, router_b, w_gu, b_gu, w_dn, b_dn):
    x = jnp.concatenate([x_prompt.reshape(TP, D), x_sample.reshape(TS, D)], axis=0)
    cond = jnp.concatenate([c_ctx[None, :], c, jnp.zeros((SUBLANE - 1 - DEC_BATCH, D), F32)], axis=0)
    mod_all = _ada_call(cond, ada_w, ada_b).reshape(DEPTH, SUBLANE, N_MOD, D)
    rope_cos, rope_sin = _rope_tables()
    cache_k4 = cache_k.reshape(DEC_BATCH, DEPTH, PAST_LEN, D)
    cache_v4 = cache_v.reshape(DEC_BATCH, DEPTH, PAST_LEN, D)

    kv_cache, c_state, ns, ms = (), (), [], []
    for l in range(DEPTH):
        lam_init = 0.8 - 0.6 * math.exp(-0.3 * l)
        mod = mod_all[l]
        w = w_in[l]
        w_main = jnp.concatenate([w[:, :GATE_OFF], w[:, GL_OFF:]], axis=1).astype(BF16)
        b_main = jnp.concatenate([b_in[l, :GATE_OFF], b_in[l, GL_OFF:]])[None, :]
        w_gate = jnp.pad(w[:, GATE_OFF:GL_OFF], ((0, 0), (0, LANE - 16))).astype(BF16)
        b_gate = jnp.pad(b_in[l, GATE_OFF:GL_OFF], (0, LANE - 16))[None, :]
        proj, gates = _inproj_call(x, mod, norm1_g[l][None, :], w_main, b_main, w_gate, b_gate)

        g4 = gates[:, :16].reshape(T, 2, 2, M_HEADS)
        gates_col = jnp.transpose(g4, (3, 0, 1, 2)).reshape(M_HEADS, T, 4)
        gates_row = jnp.transpose(g4, (3, 1, 2, 0)).reshape(M_HEADS, 4, T)

        lamp = jnp.stack([lam_q1[l], lam_k1[l], lam_q2[l], lam_k2[l]])
        qg2 = jnp.tile(qn_g[l], 2)[None, :]
        kg2 = jnp.tile(kn_g[l], 2)[None, :]
        sg = subln_g[l][None, :]
        oa_p, *kv_cache = _attn_ctx_call(proj, lamp, qg2, kg2, sg, lam_init, l, tuple(kv_cache))
        oa_s = _attn_lat_call(proj, cache_k4, cache_v4, l, rope_cos, rope_sin, lamp, qg2, kg2, sg, lam_init)

        cw, cb, cg, cbeta = conv_w[l], conv_b[l][None, :], cln_g[l][None, :], cln_b[l][None, :]
        ob = _conv_call(proj, cw, cb, cg, cbeta)

        mg = mnorm_g[l][None, :]
        oc_p, c_new, nm_new = _mlstm_call(proj, gates_col, gates_row, mg, SEQ, BATCH, 0, l, carried=c_state)
        c_state = (c_new,)
        (oc_s,) = _mlstm_call(proj, gates_col, gates_row, mg, DEC_SEQ, DEC_BATCH, TP // DEC_SEQ, l,
                              states=(state_C, state_n, state_m))

        rw = jnp.pad(router_w[l], ((0, 0), (0, LANE - N_EXPERTS)))
        rwh = rw.astype(BF16)
        rwl = (rw - rwh.astype(F32)).astype(BF16)
        rb = jnp.pad(router_b[l], (0, LANE - N_EXPERTS))[None, :]
        x1, xn2, route, counts = _merge_call(
            (oa_p, oa_s), ob, (oc_p, oc_s), proj, x, mod,
            w_br_a[l].astype(BF16), w_br_b[l].astype(BF16), w_br_c[l].astype(BF16), w_out[l].astype(BF16),
            norm2_g[l][None, :], rwh, rwl, rb)

        pos3, block_e, nused, valid = _routing_tables(route, counts)
        xs = _scatter_call(block_e, nused, pos3, xn2)
        ys = _gmm_call(block_e, valid, xs, w_gu, b_gu, w_dn, b_dn, l)
        x = _combine_call(pos3, ys, x1, route, mod)

        ns.append(jnp.transpose(nm_new[..., :M_DIM], (0, 2, 1, 3)))
        ms.append(jnp.transpose(nm_new[..., M_DIM], (0, 2, 1)))

    y_p = x[:TP].reshape(BATCH, SEQ, D)
    y_s = x[TP:].reshape(DEC_BATCH, DEC_SEQ, D)
    new_k = kv_cache[0].reshape(BATCH, DEPTH, SEQ, A_HEADS, 2 * A_DIM)
    new_v = kv_cache[1].reshape(BATCH, DEPTH, SEQ, A_HEADS, A_VDIM)
    return (y_p, y_s, new_k, new_v, c_state[0], jnp.stack(ns, axis=1), jnp.stack(ms, axis=1))
```

```python
import functools
import math

import jax
import jax.numpy as jnp
import numpy as np
from jax import lax
from jax.experimental import pallas as pl
from jax.experimental.pallas import tpu as pltpu

F32 = jnp.float32
BF16 = jnp.bfloat16

D = 1024
BATCH = 32
SEQ = 256
DEPTH = 2
DEC_BATCH = 4
DEC_SEQ = 1024
PAST_LEN = 512
GRID_W = 64
EPS = 1e-6
A_HEADS = 8
A_DIM = 64
A_VDIM = 128
ROPE_BASE = 10000.0
CONV_WIDTH = 31
CONV_PAD = CONV_WIDTH // 2
M_HEADS = 4
M_DIM = 256
N_EXPERTS = 32
TOP_K = 4
SWIGLU_LIMIT = 7.0
SWIGLU_ALPHA = 1.702
N_MOD = 6

TP = BATCH * SEQ
TS = DEC_BATCH * DEC_SEQ
T = TP + TS
N_MAIN = 12 * D
GATE_OFF = 9 * D
GL_OFF = GATE_OFF + 16

LANE = 128
SUBLANE = 8
VMEM_LIMIT = 56 * 1024 * 1024

TM_IN = 1024
TN_IN = 1024
TM_MERGE = 512
TQ = 256
M_CHUNK = 256
CONV_TILE = 256
CONV_HALO = 16
CONV_ROWS = 32
CONV_COLS = 512
MOE_BM = 512
MOE_NB = T * TOP_K // MOE_BM + N_EXPERTS
MOE_ROWS = MOE_NB * MOE_BM
TOK_TILE = 256


def _sigmoid(x):
    return 1.0 / (1.0 + jnp.exp(-x))


def _log_sigmoid(x):
    return jnp.minimum(x, 0.0) - jnp.log1p(jnp.exp(-jnp.abs(x)))


def _dot(a, b):
    return jnp.dot(a, b, preferred_element_type=F32)


def _dot_nt(a, b):
    return lax.dot_general(a, b, (((1,), (1,)), ((), ())), preferred_element_type=F32)


def _dot_tn(a, b):
    return lax.dot_general(a, b, (((0,), (0,)), ((), ())), preferred_element_type=F32)


def _params(n_axes, vmem=None):
    return pltpu.CompilerParams(dimension_semantics=("arbitrary",) * n_axes,
                                vmem_limit_bytes=vmem)


def _ada_kernel(c_ref, w_ref, b_ref, o_ref):
    c = c_ref[...]
    s = (c * _sigmoid(c)).astype(BF16)
    o_ref[0] = _dot(s, w_ref[0].astype(BF16)) + b_ref[0]


def _ada_call(cond, ada_w, ada_b):
    n = N_MOD * D
    return pl.pallas_call(
        _ada_kernel,
        out_shape=jax.ShapeDtypeStruct((DEPTH, SUBLANE, n), F32),
        grid=(DEPTH, n // D),
        in_specs=[pl.BlockSpec((SUBLANE, D), lambda l, j: (0, 0)),
                  pl.BlockSpec((1, D, D), lambda l, j: (l, 0, j)),
                  pl.BlockSpec((1, 1, D), lambda l, j: (l, 0, j))],
        out_specs=pl.BlockSpec((1, SUBLANE, D), lambda l, j: (l, 0, j)),
        compiler_params=_params(2),
        name="ada_mod",
    )(cond, ada_w, ada_b.reshape(DEPTH, 1, n))


def _mod_row_in(i):
    return jnp.where(i < TP // TM_IN, 0, i - (TP // TM_IN - 1))


def _inproj_kernel(xp_ref, xs_ref, mod_ref, g_ref, w_ref, b_ref, wg_ref, bg_ref, o_ref, og_ref, xn_ref):
    @pl.when(pl.program_id(1) == 0)
    def _():
        x = jnp.where(pl.program_id(0) < TP // TM_IN, xp_ref[...], xs_ref[...])
        ms = jnp.mean(x * x, axis=-1, keepdims=True)
        y = x * lax.rsqrt(ms + EPS) * g_ref[...]
        xn = (y * (1.0 + mod_ref[0, 1:2, :]) + mod_ref[0, 0:1, :]).astype(BF16)
        xn_ref[...] = xn
        og_ref[...] = _dot(xn, wg_ref[...]) + bg_ref[...]

    o_ref[...] = (_dot(xn_ref[...], w_ref[...]) + b_ref[...]).astype(BF16)


def _x_specs(x_pair, tile):
    n_ctx = TP // tile
    off = n_ctx if x_pair[1].shape[0] == T else 0
    ctx = lambda i, *_: (jnp.minimum(i, n_ctx - 1), 0)
    lat = lambda i, *_: (off + jnp.maximum(i - n_ctx, 0), 0)
    return [pl.BlockSpec((tile, D), ctx), pl.BlockSpec((tile, D), lat)]


def _inproj_call(x_pair, mod, g, w_main, b_main, w_gate, b_gate):
    return pl.pallas_call(
        _inproj_kernel,
        out_shape=(jax.ShapeDtypeStruct((T, N_MAIN), BF16),
                   jax.ShapeDtypeStruct((T, LANE), F32)),
        grid=(T // TM_IN, N_MAIN // TN_IN),
        in_specs=_x_specs(x_pair, TM_IN) + [
                  pl.BlockSpec((1, N_MOD, D), lambda i, j: (_mod_row_in(i), 0, 0)),
                  pl.BlockSpec((1, D), lambda i, j: (0, 0)),
                  pl.BlockSpec((D, TN_IN), lambda i, j: (0, j)),
                  pl.BlockSpec((1, TN_IN), lambda i, j: (0, j)),
                  pl.BlockSpec((D, LANE), lambda i, j: (0, 0)),
                  pl.BlockSpec((1, LANE), lambda i, j: (0, 0))],
        out_specs=(pl.BlockSpec((TM_IN, TN_IN), lambda i, j: (i, j)),
                   pl.BlockSpec((TM_IN, LANE), lambda i, j: (i, 0))),
        scratch_shapes=[pltpu.VMEM((TM_IN, D), BF16)],
        compiler_params=_params(2, VMEM_LIMIT),
        name="in_proj",
    )(x_pair[0], x_pair[1], mod, g, w_main, b_main, w_gate, b_gate)


def _qk_norm(x, g2):
    lo = lax.broadcasted_iota(jnp.int32, x.shape, 1) < A_DIM
    x2 = x * x
    s0 = jnp.sum(jnp.where(lo, x2, 0.0), axis=-1, keepdims=True)
    s1 = jnp.sum(jnp.where(lo, 0.0, x2), axis=-1, keepdims=True)
    ms = jnp.where(lo, s0, s1) * (1.0 / A_DIM)
    return x * lax.rsqrt(ms + EPS) * g2


def _rope(x, cos, sin_signed):
    first = (lax.broadcasted_iota(jnp.int32, x.shape, 1) % 32) < 16
    partner = jnp.where(first, pltpu.roll(x, LANE - 16, 1), pltpu.roll(x, 16, 1))
    return x * cos + partner * sin_signed


def _lambda(lamp_ref, lam_init):
    l1 = jnp.sum(lamp_ref[0:1, :] * lamp_ref[1:2, :], axis=-1, keepdims=True)
    l2 = jnp.sum(lamp_ref[2:3, :] * lamp_ref[3:4, :], axis=-1, keepdims=True)
    return jnp.exp(l1) - jnp.exp(l2) + lam_init


def _softmax_times(s, vh):
    e = jnp.exp(s - jnp.max(s, axis=-1, keepdims=True))
    return _dot(e.astype(BF16), vh) / jnp.sum(e, axis=-1, keepdims=True)


def _diff_attn_head(qh, kh, vh, lam, sg, lam_init):
    lo = lax.broadcasted_iota(jnp.int32, qh.shape, 1) < A_DIM
    qs = qh * (A_DIM ** -0.5)
    q0 = jnp.where(lo, qs, 0.0).astype(BF16)
    q1 = jnp.where(lo, 0.0, qs).astype(BF16)
    o = _softmax_times(_dot_nt(q0, kh), vh) - lam * _softmax_times(_dot_nt(q1, kh), vh)
    ms = jnp.mean(o * o, axis=-1, keepdims=True)
    return o * lax.rsqrt(ms + EPS) * sg * (1.0 - lam_init)


def _attn_ctx_kernel(*refs, lam_init, n_carried):
    lamp_ref, q_ref, k_ref, v_ref, qg_ref, kg_ref, sg_ref = refs[:7]
    o_ref, kn_ref, vc_ref = refs[7 + n_carried:]
    lam = _lambda(lamp_ref, lam_init)
    n_slots = kn_ref.shape[1]
    for s in range(n_slots):
        vc_ref[0, s] = v_ref[...].astype(F32)
    for h in range(A_HEADS):
        sl = slice(h * LANE, (h + 1) * LANE)
        kn = _qk_norm(k_ref[:, sl].astype(F32), kg_ref[...])
        for s in range(n_slots):
            kn_ref[0, s, :, sl] = kn
        qn = _qk_norm(q_ref[:, sl].astype(F32), qg_ref[...])
        o = _diff_attn_head(qn, kn.astype(BF16), v_ref[:, sl], lam, sg_ref[...], lam_init)
        o_ref[:, sl] = o.astype(BF16)


def _attn_ctx_call(proj, lamp, qg2, kg2, sg, lam_init, layer, carried):
    blk = lambda c: pl.BlockSpec((SEQ, D), lambda b, c=c: (b, c))
    vec = pl.BlockSpec((1, LANE), lambda b: (0, 0))
    if carried:
        cache = pl.BlockSpec((1, 1, SEQ, D), lambda b: (b, layer, 0, 0))
    else:
        cache = pl.BlockSpec((1, DEPTH, SEQ, D), lambda b: (b, 0, 0, 0))
    cache_shape = jax.ShapeDtypeStruct((BATCH, DEPTH, SEQ, D), F32)
    n_in = 7
    return pl.pallas_call(
        functools.partial(_attn_ctx_kernel, lam_init=lam_init, n_carried=len(carried)),
        out_shape=(jax.ShapeDtypeStruct((TP, D), BF16), cache_shape, cache_shape),
        grid=(BATCH,),
        in_specs=[pl.BlockSpec((4, A_DIM), lambda b: (0, 0)), blk(0), blk(1), blk(2), vec, vec, vec]
                 + [pl.BlockSpec(memory_space=pl.ANY)] * len(carried),
        out_specs=(pl.BlockSpec((SEQ, D), lambda b: (b, 0)), cache, cache),
        input_output_aliases={n_in + j: 1 + j for j in range(len(carried))},
        compiler_params=_params(1, VMEM_LIMIT),
        name="attn_ctx",
    )(lamp, proj, proj, proj, qg2, kg2, sg, *carried)


def _attn_lat_kernel(lamp_ref, q_ref, k_ref, v_ref, kc_ref, vc_ref, cos_ref, sin_ref, cosq_ref, sinq_ref,
                     qg_ref, kg_ref, sg_ref, o_ref, kall_ref, vall_ref, *, lam_init):
    @pl.when(pl.program_id(1) == 0)
    def _():
        for h in range(A_HEADS):
            sl = slice(h * LANE, (h + 1) * LANE)
            kn = _rope(_qk_norm(k_ref[:, sl].astype(F32), kg_ref[...]), cos_ref[...], sin_ref[...])
            kall_ref[0:DEC_SEQ, sl] = kn.astype(BF16)
        vall_ref[0:DEC_SEQ, :] = v_ref[...]
        kall_ref[DEC_SEQ:, :] = kc_ref[0, 0].astype(BF16)
        vall_ref[DEC_SEQ:, :] = vc_ref[0, 0].astype(BF16)

    lam = _lambda(lamp_ref, lam_init)
    for h in range(A_HEADS):
        sl = slice(h * LANE, (h + 1) * LANE)
        qn = _rope(_qk_norm(q_ref[:, sl].astype(F32), qg_ref[...]), cosq_ref[...], sinq_ref[...])
        o = _diff_attn_head(qn, kall_ref[:, sl], vall_ref[:, sl], lam, sg_ref[...], lam_init)
        o_ref[:, sl] = o.astype(BF16)


def _attn_lat_call(proj, cache_k, cache_v, layer, rope_cos, rope_sin, lamp, qg2, kg2, sg, lam_init):
    nq = DEC_SEQ // TQ
    r0 = TP // DEC_SEQ
    vec = pl.BlockSpec((1, LANE), lambda b, i: (0, 0))
    ctx = pl.BlockSpec((1, 1, PAST_LEN, D), lambda b, i: (b, layer, 0, 0))
    full = lambda c: pl.BlockSpec((DEC_SEQ, D), lambda b, i, c=c: (r0 + b, c))
    return pl.pallas_call(
        functools.partial(_attn_lat_kernel, lam_init=lam_init),
        out_shape=jax.ShapeDtypeStruct((TS, D), BF16),
        grid=(DEC_BATCH, nq),
        in_specs=[pl.BlockSpec((4, A_DIM), lambda b, i: (0, 0)),
                  pl.BlockSpec((TQ, D), lambda b, i: ((TP // TQ) + b * nq + i, 0)),
                  full(1), full(2), ctx, ctx,
                  pl.BlockSpec((DEC_SEQ, LANE), lambda b, i: (0, 0)),
                  pl.BlockSpec((DEC_SEQ, LANE), lambda b, i: (0, 0)),
                  pl.BlockSpec((TQ, LANE), lambda b, i: (i, 0)),
                  pl.BlockSpec((TQ, LANE), lambda b, i: (i, 0)),
                  vec, vec, vec],
        out_specs=pl.BlockSpec((TQ, D), lambda b, i: (b * nq + i, 0)),
        scratch_shapes=[pltpu.VMEM((DEC_SEQ + PAST_LEN, D), BF16),
                        pltpu.VMEM((DEC_SEQ + PAST_LEN, D), BF16)],
        compiler_params=_params(2, VMEM_LIMIT),
        name="attn_lat",
    )(lamp, proj, proj, proj, cache_k, cache_v, rope_cos, rope_sin, rope_cos, rope_sin, qg2, kg2, sg)


def _rope_tables():
    t = np.arange(DEC_SEQ)
    row, col = t // GRID_W, t % GRID_W
    lane = np.arange(LANE)
    jj = lane % A_DIM
    freq = ROPE_BASE ** (-(jj % 16).astype(np.float64) / 16.0)
    pos = np.where((jj // 32)[None, :] == 0, row[:, None], col[:, None]).astype(np.float64)
    ang = pos.astype(np.float32) * freq.astype(np.float32)[None, :]
    sign = np.where((jj % 32) < 16, -1.0, 1.0).astype(np.float32)
    return jnp.cos(jnp.asarray(ang)), jnp.sin(jnp.asarray(ang)) * jnp.asarray(sign)[None, :]


def _conv_neighbours(i):
    j = (i - TP // CONV_TILE) % (DEC_SEQ // CONV_TILE)
    lat = i >= TP // CONV_TILE
    return lat & (j > 0), lat & (j < DEC_SEQ // CONV_TILE - 1)


def _conv_kernel(a_ref, gate_ref, ap_ref, gp_ref, an_ref, gn_ref, w_ref, b_ref, g_ref, beta_ref, o_ref,
                 hp_ref, hs_ref, acc_ref):
    halo = CONV_HALO
    seq = CONV_TILE
    has_prev, has_next = _conv_neighbours(pl.program_id(0))
    glu = lambda a, g: a[...].astype(F32) * _sigmoid(g[...].astype(F32))
    hp_ref[0:halo, :] = jnp.where(has_prev, glu(ap_ref, gp_ref), 0.0)
    hp_ref[halo + seq:, :] = jnp.where(has_next, glu(an_ref, gn_ref), 0.0)
    hp_ref[halo:halo + seq, :] = glu(a_ref, gate_ref)

    srows = hs_ref.shape[1]
    for s in range(SUBLANE):
        hs_ref[s] = hp_ref[s:s + srows, :]

    for t0 in range(0, seq, CONV_ROWS):
        for c0 in range(0, D, CONV_COLS):
            acc = jnp.zeros((CONV_ROWS, CONV_COLS), F32)
            for j in range(CONV_WIDTH):
                r = halo - CONV_PAD + t0 + j
                s = r % SUBLANE
                acc = acc + (w_ref[j:j + 1, c0:c0 + CONV_COLS]
                             * hs_ref[s, r - s:r - s + CONV_ROWS, c0:c0 + CONV_COLS])
            acc_ref[t0:t0 + CONV_ROWS, c0:c0 + CONV_COLS] = acc + b_ref[:, c0:c0 + CONV_COLS]

    h = acc_ref[...]
    mu = jnp.mean(h, axis=-1, keepdims=True)
    hc = h - mu
    y = hc * lax.rsqrt(jnp.mean(hc * hc, axis=-1, keepdims=True) + EPS) * g_ref[...] + beta_ref[...]
    o_ref[...] = (y * _sigmoid(y)).astype(BF16)


def _conv_call(proj, conv_w, conv_b, cln_g, cln_b):
    vec = pl.BlockSpec((1, D), lambda i: (0, 0))
    per = CONV_TILE // CONV_HALO
    cur = lambda c: pl.BlockSpec((CONV_TILE, D), lambda i, c=c: (i, c))
    prev = lambda c: pl.BlockSpec((CONV_HALO, D), lambda i, c=c: (jnp.maximum(i * per - 1, 0), c))
    nxt = lambda c: pl.BlockSpec((CONV_HALO, D), lambda i, c=c: (jnp.minimum((i + 1) * per, T // CONV_HALO - 1), c))
    return pl.pallas_call(
        _conv_kernel,
        out_shape=jax.ShapeDtypeStruct((T, D), BF16),
        grid=(T // CONV_TILE,),
        in_specs=[cur(3), cur(4), prev(3), prev(4), nxt(3), nxt(4),
                  pl.BlockSpec((CONV_WIDTH, D), lambda i: (0, 0)),
                  vec, vec, vec],
        out_specs=pl.BlockSpec((CONV_TILE, D), lambda i: (i, 0)),
        scratch_shapes=[pltpu.VMEM((CONV_TILE + 2 * CONV_HALO, D), F32),
                        pltpu.VMEM((SUBLANE, CONV_TILE + 2 * CONV_HALO - SUBLANE, D), F32),
                        pltpu.VMEM((CONV_TILE, D), F32)],
        compiler_params=_params(1, VMEM_LIMIT),
        name="conformer_conv",
    )(proj, proj, proj, proj, proj, proj, conv_w, conv_b, cln_g, cln_b)


def _mlstm_kernel(*refs, seq, has_state, emit_state, layer, n_carried):
    if has_state:
        (m0_ref, q_ref, k_ref, v_ref, og_ref, gc_ref, gr_ref, mg_ref, c0_ref, n0_ref) = refs[:10]
        rest = refs[10:]
    else:
        (q_ref, k_ref, v_ref, og_ref, gc_ref, gr_ref, mg_ref) = refs[:7]
        rest = refs[7 + n_carried:]
    if emit_state:
        o_ref, cout_ref, nm_ref, hacc_ref, c_ref = rest
    else:
        o_ref, hacc_ref, c_ref = rest

    b_id = pl.program_id(0)
    h_id = pl.program_id(1)
    L = M_CHUNK
    nc = seq // L
    t_idx = lax.broadcasted_iota(jnp.int32, (L, L), 0)
    s_idx = lax.broadcasted_iota(jnp.int32, (L, L), 1)

    for d in range(2):
        mask = (s_idx <= t_idx) if d == 0 else (s_idx >= t_idx)
        mask_t = (t_idx <= s_idx) if d == 0 else (t_idx >= s_idx)
        last = L - 1 if d == 0 else 0
        if has_state:
            c_ref[...] = c0_ref[0, 0, d, 0]
            n = n0_ref[0, 0, d, pl.ds(h_id, 1), :]
            m = jnp.full((1, 1), m0_ref[b_id, layer, d, h_id], F32)
        else:
            n = jnp.zeros((1, M_DIM), F32)
            m = jnp.zeros((1, 1), F32)
        chunks = range(nc) if d == 0 else range(nc - 1, -1, -1)
        for ci, c in enumerate(chunks):
            rows = slice(c * L, (c + 1) * L)
            first_chunk = ci == 0
            last_chunk = ci == nc - 1
            qb = q_ref[rows, :]
            q = qb.astype(F32)
            k = k_ref[rows, :].astype(F32) * (M_DIM ** -0.5)
            kb, vb = k.astype(BF16), v_ref[rows, :]
            gi_col = gc_ref[0, rows, d:d + 1]
            gi_row = gr_ref[0, d:d + 1, rows]
            lf_col = _log_sigmoid(gc_ref[0, rows, 2 + d:3 + d])
            lf_row = _log_sigmoid(gr_ref[0, 2 + d:3 + d, rows])
            b_col = jnp.sum(jnp.where(mask, lf_row, 0.0), axis=1, keepdims=True)
            b_row = jnp.sum(jnp.where(mask_t, lf_col, 0.0), axis=0, keepdims=True)
            dmat = jnp.where(mask, b_col - b_row + gi_row, -jnp.inf)
            inter = b_col + m
            mt = jnp.maximum(inter, jnp.max(dmat, axis=1, keepdims=True))
            w = jnp.exp(dmat - mt)
            s_inter = jnp.exp(inter - mt)
            sw = _dot_nt(qb, kb) * w
            num = _dot(sw.astype(BF16), vb)
            den = jnp.sum(sw, axis=1, keepdims=True)
            if has_state or not first_chunk:
                num = num + s_inter * _dot(qb, c_ref[...].astype(BF16))
                den = den + s_inter * jnp.sum(q * n, axis=1, keepdims=True)
            hout = num / jnp.maximum(jnp.abs(den), jnp.exp(-mt))
            if d == 0:
                hacc_ref[rows, :] = hout
            else:
                hacc_ref[rows, :] = hacc_ref[rows, :] + hout
            if emit_state or not last_chunk:
                m_new = mt[last:last + 1, :]
                btot = b_col[last:last + 1, :]
                decay = jnp.exp(btot + m - m_new)
                wk_col = jnp.exp(btot - b_col + gi_col - m_new)
                kw = k * wk_col
                upd = _dot_tn(kw.astype(BF16), vb)
                nsum = jnp.sum(kw, axis=0, keepdims=True)
                if has_state or not first_chunk:
                    c_ref[...] = decay * c_ref[...] + upd
                    n = decay * n + nsum
                else:
                    c_ref[...] = upd
                    n = nsum
                m = m_new
        if emit_state:
            for s in range(cout_ref.shape[1]):
                cout_ref[0, s, d, 0] = c_ref[...]
            nm_ref[0, 0, d:d + 1, 0:M_DIM] = n
            nm_ref[0, 0, d:d + 1, M_DIM:] = jnp.broadcast_to(m, (1, LANE))

    hm = hacc_ref[...]
    ms = jnp.mean(hm * hm, axis=-1, keepdims=True)
    y = hm * lax.rsqrt(ms + EPS) * mg_ref[...]
    o_ref[...] = (y * _sigmoid(og_ref[...].astype(F32))).astype(BF16)


def _mlstm_call(proj, gates_col, gates_row, mnorm_g, seq, nb, row0, layer, states=None, carried=()):
    has_state = states is not None
    emit_state = not has_state
    aliases = {}
    col = lambda c0: pl.BlockSpec((seq, M_DIM), lambda b, h, c0=c0: (row0 + b, c0 + h))
    in_specs = [col(20), col(24), col(28), col(32),
                pl.BlockSpec((1, seq, 4), lambda b, h: (h, row0 + b, 0)),
                pl.BlockSpec((1, 4, seq), lambda b, h: (h, 0, row0 + b)),
                pl.BlockSpec((1, M_DIM), lambda b, h: (0, h))]
    args = [proj, proj, proj, proj, gates_col, gates_row, mnorm_g]
    if has_state:
        state_c, state_n, state_m = states
        in_specs = [pl.BlockSpec(memory_space=pltpu.SMEM)] + in_specs + [
            pl.BlockSpec((1, 1, 2, 1, M_DIM, M_DIM), lambda b, h: (b, layer, 0, h, 0, 0)),
            pl.BlockSpec((1, 1, 2, M_HEADS, M_DIM), lambda b, h: (b, layer, 0, 0, 0))]
        args = [state_m] + args + [state_c, state_n]
    out_shape = [jax.ShapeDtypeStruct((nb * seq, D), BF16)]
    out_specs = [pl.BlockSpec((seq, M_DIM), lambda b, h: (b, h))]
    if emit_state:
        out_shape += [jax.ShapeDtypeStruct((nb, DEPTH, 2, M_HEADS, M_DIM, M_DIM), F32),
                      jax.ShapeDtypeStruct((nb, M_HEADS, 2, M_DIM + LANE), F32)]
        if carried:
            c_spec = pl.BlockSpec((1, 1, 2, 1, M_DIM, M_DIM), lambda b, h: (b, layer, 0, h, 0, 0))
        else:
            c_spec = pl.BlockSpec((1, DEPTH, 2, 1, M_DIM, M_DIM), lambda b, h: (b, 0, 0, h, 0, 0))
        out_specs += [c_spec, pl.BlockSpec((1, 1, 2, M_DIM + LANE), lambda b, h: (b, h, 0, 0))]
        aliases = {len(args) + j: 1 + j for j in range(len(carried))}
        in_specs = in_specs + [pl.BlockSpec(memory_space=pl.ANY)] * len(carried)
        args = args + list(carried)
    return pl.pallas_call(
        functools.partial(_mlstm_kernel, seq=seq, has_state=has_state, emit_state=emit_state, layer=layer,
                          n_carried=len(carried)),
        out_shape=tuple(out_shape),
        grid=(nb, M_HEADS),
        in_specs=in_specs,
        out_specs=tuple(out_specs),
        input_output_aliases=aliases,
        scratch_shapes=[pltpu.VMEM((seq, M_DIM), F32), pltpu.VMEM((M_DIM, M_DIM), F32)],
        compiler_params=_params(2, VMEM_LIMIT),
        name="mlstm",
    )(*args)


def _mod_row_merge(i):
    per = DEC_SEQ // TM_MERGE
    return jnp.where(i < TP // TM_MERGE, 0, 1 + (i - TP // TM_MERGE) // per)


def _split3(x):
    hi = x.astype(BF16)
    lo = (x - hi.astype(F32)).astype(BF16)
    return hi, lo


def _merge_kernel(oap, oas, ob_ref, ocp, ocs, gl_ref, xp_ref, xs_ref, mod_ref, wa_ref, wb_ref, wc_ref, wo_ref,
                  g2_ref, rwh_ref, rwl_ref, rb_ref, x1_ref, xn_ref, route_ref, cnt_ref, carry_ref):
    i = pl.program_id(0)
    is_ctx = i < TP // TM_MERGE

    @pl.when(i == 0)
    def _():
        carry_ref[...] = jnp.zeros_like(carry_ref)

    def branch(p_ref, s_ref, w_ref, c):
        o = jnp.where(is_ctx, p_ref[...], s_ref[...])
        return _sigmoid(gl_ref[:, c * D:(c + 1) * D].astype(F32)) * _dot(o, w_ref[...])

    merged = (branch(oap, oas, wa_ref, 0)
              + _sigmoid(gl_ref[:, D:2 * D].astype(F32)) * _dot(ob_ref[...], wb_ref[...])
              + branch(ocp, ocs, wc_ref, 2))
    out = _dot(merged.astype(BF16), wo_ref[...])
    x1 = jnp.where(is_ctx, xp_ref[...], xs_ref[...]) + mod_ref[0, 2:3, :] * out
    x1_ref[...] = x1
    ms = jnp.mean(x1 * x1, axis=-1, keepdims=True)
    xn = x1 * lax.rsqrt(ms + EPS) * g2_ref[...] * (1.0 + mod_ref[0, 4:5, :]) + mod_ref[0, 3:4, :]
    xn_ref[...] = xn

    xh, xl = _split3(xn)
    logits = _dot(xh, rwh_ref[...]) + _dot(xh, rwl_ref[...]) + _dot(xl, rwh_ref[...]) + rb_ref[...]
    lane = lax.broadcasted_iota(jnp.int32, logits.shape, 1)
    lanef = lane.astype(F32)
    cur = jnp.where(lane < N_EXPERTS, logits, -jnp.inf)
    vals, idxs = [], []
    for _ in range(TOP_K):
        mx = jnp.max(cur, axis=-1, keepdims=True)
        idx = jnp.min(jnp.where(cur == mx, lanef, float(LANE)), axis=-1, keepdims=True)
        vals.append(mx)
        idxs.append(idx)
        cur = jnp.where(lanef == idx, -jnp.inf, cur)
    exps = [jnp.exp(v - vals[0]) for v in vals]
    den = exps[0] + exps[1] + exps[2] + exps[3]

    onehot = jnp.zeros(logits.shape, F32)
    for idx in idxs:
        onehot = onehot + jnp.where(lanef == idx, 1.0, 0.0)
    r_idx = lax.broadcasted_iota(jnp.int32, (TM_MERGE, TM_MERGE), 0)
    c_idx = lax.broadcasted_iota(jnp.int32, (TM_MERGE, TM_MERGE), 1)
    tril = jnp.where(c_idx <= r_idx, 1.0, 0.0).astype(BF16)
    incl = _dot(tril, onehot.astype(BF16))
    tot = carry_ref[...] + incl
    route = jnp.zeros(logits.shape, F32)
    for k in range(TOP_K):
        rank = jnp.sum(jnp.where(lanef == idxs[k], tot, 0.0), axis=-1, keepdims=True) - 1.0
        route = jnp.where(lane == k, idxs[k], route)
        route = jnp.where(lane == TOP_K + k, exps[k] / den, route)
        route = jnp.where(lane == 2 * TOP_K + k, rank, route)
    route_ref[...] = route
    carry_ref[...] = tot[TM_MERGE - 1:TM_MERGE, :]
    cnt_ref[...] = jnp.broadcast_to(tot[TM_MERGE - 1:TM_MERGE, :], cnt_ref.shape)


def _merge_call(oa, ob, oc, proj, x_pair, mod, wa, wb, wc, wo, g2, rwh, rwl, rb):
    npt = TP // TM_MERGE
    pblk = pl.BlockSpec((TM_MERGE, D), lambda i: (jnp.minimum(i, npt - 1), 0))
    sblk = pl.BlockSpec((TM_MERGE, D), lambda i: (jnp.maximum(i - npt, 0), 0))
    wblk = pl.BlockSpec((D, D), lambda i: (0, 0))
    rblk = pl.BlockSpec((D, LANE), lambda i: (0, 0))
    return pl.pallas_call(
        _merge_kernel,
        out_shape=(jax.ShapeDtypeStruct((T, D), F32),
                   jax.ShapeDtypeStruct((T, D), F32),
                   jax.ShapeDtypeStruct((T, LANE), F32),
                   jax.ShapeDtypeStruct((SUBLANE, LANE), F32)),
        grid=(T // TM_MERGE,),
        in_specs=[pblk, sblk, pl.BlockSpec((TM_MERGE, D), lambda i: (i, 0)), pblk, sblk,
                  pl.BlockSpec((TM_MERGE, 3 * D), lambda i: (i, 3))]
                 + _x_specs(x_pair, TM_MERGE) + [
                  pl.BlockSpec((1, N_MOD, D), lambda i: (_mod_row_merge(i), 0, 0)),
                  wblk, wblk, wblk, wblk,
                  pl.BlockSpec((1, D), lambda i: (0, 0)),
                  rblk, rblk,
                  pl.BlockSpec((1, LANE), lambda i: (0, 0))],
        out_specs=(pl.BlockSpec((TM_MERGE, D), lambda i: (i, 0)),
                   pl.BlockSpec((TM_MERGE, D), lambda i: (i, 0)),
                   pl.BlockSpec((TM_MERGE, LANE), lambda i: (i, 0)),
                   pl.BlockSpec((SUBLANE, LANE), lambda i: (0, 0))),
        scratch_shapes=[pltpu.VMEM((1, LANE), F32)],
        compiler_params=_params(1, VMEM_LIMIT),
        name="merge_router",
    )(oa[0], oa[1], ob, oc[0], oc[1], proj, x_pair[0], x_pair[1], mod, wa, wb, wc, wo, g2, rwh, rwl, rb)


def _row_copy(src_ref, src_row, dst_ref, dst_row, sem):
    return pltpu.make_async_copy(src_ref.at[pl.ds(src_row, 1), :], dst_ref.at[pl.ds(dst_row, 1), :], sem)


def _tile_copy(src_ref, dst_ref, sem):
    return pltpu.make_async_copy(src_ref, dst_ref, sem)


def _scatter_kernel(be_ref, nused_ref, pos_ref, x_ref, xs_ref, zeros_ref, sem):
    i = pl.program_id(0)

    @pl.when(i == 0)
    def _():
        zeros_ref[...] = jnp.zeros_like(zeros_ref)

        def partial_block(b):
            nxt = be_ref[jnp.minimum(b + 1, MOE_NB - 1)]
            return (b >= nused_ref[0] - 1) | (nxt != be_ref[b])

        def fill(b, carry):
            @pl.when(partial_block(b))
            def _():
                pltpu.make_async_copy(zeros_ref, xs_ref.at[pl.ds(b * MOE_BM, MOE_BM), :], sem).start()
            return carry

        def fill_wait(b, carry):
            @pl.when(partial_block(b))
            def _():
                pltpu.make_async_copy(zeros_ref, xs_ref.at[pl.ds(b * MOE_BM, MOE_BM), :], sem).wait()
            return carry

        lax.fori_loop(0, MOE_NB, fill, 0)
        lax.fori_loop(0, MOE_NB, fill_wait, 0)

    def start(r, carry):
        for k in range(TOP_K):
            _row_copy(x_ref, r, xs_ref, pos_ref[0, 0, r * TOP_K + k], sem).start()
        return carry

    lax.fori_loop(0, TOK_TILE, start, 0)
    for k in range(TOP_K):
        _tile_copy(x_ref, xs_ref.at[pl.ds(0, TOK_TILE), :], sem).wait()


def _scatter_call(block_e, nused, pos3, xn):
    grid_spec = pltpu.PrefetchScalarGridSpec(
        num_scalar_prefetch=2,
        grid=(T // TOK_TILE,),
        in_specs=[pl.BlockSpec((1, 1, TOK_TILE * TOP_K), lambda i, be, nu: (i, 0, 0), memory_space=pltpu.SMEM),
                  pl.BlockSpec((TOK_TILE, D), lambda i, be, nu: (i, 0))],
        out_specs=pl.BlockSpec(memory_space=pl.ANY),
        scratch_shapes=[pltpu.VMEM((MOE_BM, D), F32), pltpu.SemaphoreType.DMA],
    )
    return pl.pallas_call(
        _scatter_kernel,
        out_shape=jax.ShapeDtypeStruct((MOE_ROWS, D), F32),
        grid_spec=grid_spec,
        compiler_params=_params(1),
        name="moe_scatter",
    )(block_e, nused, pos3, xn)


def _gmm_kernel(be_ref, valid_ref, x_ref, wgu_ref, bgu_ref, wdn_ref, bdn_ref, y_ref, wgu_bf, wdn_bf):
    i = pl.program_id(0)
    prev = be_ref[jnp.maximum(i - 1, 0)]
    valid = valid_ref[i]

    @pl.when((i == 0) | (be_ref[i] != prev))
    def _():
        wgu_bf[...] = wgu_ref[0, 0].astype(BF16)
        wdn_bf[...] = wdn_ref[0, 0].astype(BF16)

    def expert_mlp(rows):
        gu = _dot(x_ref[0:rows, :].astype(BF16), wgu_bf[...]) + bgu_ref[0, 0]
        g = jnp.minimum(gu[:, :D], SWIGLU_LIMIT)
        u = jnp.clip(gu[:, D:], -SWIGLU_LIMIT, SWIGLU_LIMIT)
        hdn = (u + 1.0) * (g * _sigmoid(SWIGLU_ALPHA * g))
        y_ref[0:rows, :] = _dot(hdn.astype(BF16), wdn_bf[...]) + bdn_ref[0, 0]
        if rows < MOE_BM:
            y_ref[rows:, :] = jnp.zeros((MOE_BM - rows, D), F32)

    @pl.when(valid > MOE_BM // 2)
    def _():
        expert_mlp(MOE_BM)

    @pl.when((valid > 0) & (valid <= MOE_BM // 2))
    def _():
        expert_mlp(MOE_BM // 2)

    @pl.when(valid == 0)
    def _():
        y_ref[...] = jnp.zeros_like(y_ref)


def _gmm_call(block_e, valid, xs, w_gu, b_gu, w_dn, b_dn, layer):
    grid_spec = pltpu.PrefetchScalarGridSpec(
        num_scalar_prefetch=2,
        grid=(MOE_NB,),
        in_specs=[pl.BlockSpec((MOE_BM, D), lambda i, be, nu: (i, 0)),
                  pl.BlockSpec((1, 1, D, 2 * D), lambda i, be, nu: (layer, be[i], 0, 0)),
                  pl.BlockSpec((1, 1, 1, 2 * D), lambda i, be, nu: (layer, be[i], 0, 0)),
                  pl.BlockSpec((1, 1, D, D), lambda i, be, nu: (layer, be[i], 0, 0)),
                  pl.BlockSpec((1, 1, 1, D), lambda i, be, nu: (layer, be[i], 0, 0))],
        out_specs=pl.BlockSpec((MOE_BM, D), lambda i, be, nu: (i, 0)),
        scratch_shapes=[pltpu.VMEM((D, 2 * D), BF16), pltpu.VMEM((D, D), BF16)],
    )
    return pl.pallas_call(
        _gmm_kernel,
        out_shape=jax.ShapeDtypeStruct((MOE_ROWS, D), F32),
        grid_spec=grid_spec,
        compiler_params=_params(1, VMEM_LIMIT),
        name="moe_experts",
    )(block_e, valid, xs, w_gu, b_gu.reshape(DEPTH, N_EXPERTS, 1, 2 * D), w_dn,
      b_dn.reshape(DEPTH, N_EXPERTS, 1, D))


def _mod_row_tok(i):
    per = DEC_SEQ // TOK_TILE
    return jnp.where(i < TP // TOK_TILE, 0, 1 + (i - TP // TOK_TILE) // per)


def _combine_kernel(pos_ref, pos_next_ref, ys_ref, x1_ref, route_ref, mod_ref, o_ref, buf_ref, sems, *, n_tiles):
    i = pl.program_id(0)
    last = n_tiles - 1
    slot = i % 2

    def gather(p_ref, s):
        def start(r, carry):
            for k in range(TOP_K):
                _row_copy(ys_ref, p_ref[0, 0, r * TOP_K + k], buf_ref.at[s, k], r, sems.at[s]).start()
            return carry

        lax.fori_loop(0, TOK_TILE, start, 0)

    @pl.when(i == 0)
    def _():
        gather(pos_ref, slot)

    @pl.when(i < last)
    def _():
        gather(pos_next_ref, 1 - slot)

    for k in range(TOP_K):
        _tile_copy(ys_ref.at[pl.ds(0, TOK_TILE), :], buf_ref.at[slot, k], sems.at[slot]).wait()

    acc = jnp.zeros((TOK_TILE, D), F32)
    for k in range(TOP_K):
        acc = acc + route_ref[:, TOP_K + k:TOP_K + k + 1] * buf_ref[slot, k]
    o_ref[...] = x1_ref[...] + mod_ref[0, 5:6, :] * acc


def _combine_call(pos3, ys, x1, route, mod, tile0=0, n_tiles=T // TOK_TILE):
    pos_spec = lambda f: pl.BlockSpec((1, 1, TOK_TILE * TOP_K), lambda i: (tile0 + f(i), 0, 0),
                                      memory_space=pltpu.SMEM)
    return pl.pallas_call(
        functools.partial(_combine_kernel, n_tiles=n_tiles),
        out_shape=jax.ShapeDtypeStruct((n_tiles * TOK_TILE, D), F32),
        grid=(n_tiles,),
        in_specs=[pos_spec(lambda i: i), pos_spec(lambda i: jnp.minimum(i + 1, n_tiles - 1)),
                  pl.BlockSpec(memory_space=pl.ANY),
                  pl.BlockSpec((TOK_TILE, D), lambda i: (tile0 + i, 0)),
                  pl.BlockSpec((TOK_TILE, LANE), lambda i: (tile0 + i, 0)),
                  pl.BlockSpec((1, N_MOD, D), lambda i: (_mod_row_tok(tile0 + i), 0, 0))],
        out_specs=pl.BlockSpec((TOK_TILE, D), lambda i: (i, 0)),
        scratch_shapes=[pltpu.VMEM((2, TOP_K, TOK_TILE, D), F32), pltpu.SemaphoreType.DMA((2,))],
        compiler_params=_params(1, VMEM_LIMIT),
        name="moe_combine",
    )(pos3, pos3, ys, x1, route, mod)


def _pos_kernel(route_ref, start_ref, o_ref):
    r = route_ref[...]
    lane = lax.broadcasted_iota(jnp.int32, r.shape, 1)
    lanef = lane.astype(F32)
    out = jnp.zeros(r.shape, jnp.int32)
    for k in range(TOP_K):
        first = jnp.sum(jnp.where(lanef == r[:, k:k + 1], start_ref[...], 0.0), axis=-1, keepdims=True)
        pos = first + r[:, 2 * TOP_K + k:2 * TOP_K + k + 1]
        out = jnp.where(lane == k, pos.astype(jnp.int32), out)
    o_ref[...] = out


def _pos_call(route, pad_start):
    return pl.pallas_call(
        _pos_kernel,
        out_shape=jax.ShapeDtypeStruct((T, LANE), jnp.int32),
        grid=(T // TM_IN,),
        in_specs=[pl.BlockSpec((TM_IN, LANE), lambda i: (i, 0)),
                  pl.BlockSpec((1, LANE), lambda i: (0, 0))],
        out_specs=pl.BlockSpec((TM_IN, LANE), lambda i: (i, 0)),
        compiler_params=_params(1),
        name="moe_rows",
    )(route, pad_start)


def _routing_tables(route, counts):
    cnt = counts[0, :N_EXPERTS].astype(jnp.int32)
    padded = (cnt + MOE_BM - 1) // MOE_BM * MOE_BM
    pad_end = jnp.cumsum(padded)
    pad_start = pad_end - padded
    starts = jnp.arange(MOE_NB, dtype=jnp.int32) * MOE_BM
    block_e = jnp.minimum(jnp.sum((pad_end[None, :] <= starts[:, None]).astype(jnp.int32), axis=1),
                          N_EXPERTS - 1)
    valid = jnp.clip((pad_start + cnt)[block_e] - starts, 0, MOE_BM)
    nused = (pad_end[-1:] // MOE_BM).astype(jnp.int32)
    start_row = jnp.pad(pad_start.astype(F32), (0, LANE - N_EXPERTS))[None, :]
    pos = _pos_call(route, start_row)[:, :TOP_K]
    return pos.reshape(T // TOK_TILE, 1, TOK_TILE * TOP_K), block_e, nused, valid


def kernel(x_prompt, x_sample, c, c_ctx, cache_k, cache_v, state_C, state_n, state_m, ada_w, ada_b, norm1_g, norm2_g, w_in, b_in, qn_g, kn_g, lam_q1, lam_k1, lam_q2, lam_k2, subln_g, conv_w, conv_b, cln_g, cln_b, mnorm_g, w_br_a, w_br_b, w_br_c, w_out, router_w, router_b, w_gu, b_gu, w_dn, b_dn):
    x_pair = (x_prompt.reshape(TP, D), x_sample.reshape(TS, D))
    cond = jnp.concatenate([c_ctx[None, :], c, jnp.zeros((SUBLANE - 1 - DEC_BATCH, D), F32)], axis=0)
    mod_all = _ada_call(cond, ada_w, ada_b).reshape(DEPTH, SUBLANE, N_MOD, D)
    rope_cos, rope_sin = _rope_tables()
    cache_k4 = cache_k.reshape(DEC_BATCH, DEPTH, PAST_LEN, D)
    cache_v4 = cache_v.reshape(DEC_BATCH, DEPTH, PAST_LEN, D)

    kv_cache, c_state, ns, ms = (), (), [], []
    for l in range(DEPTH):
        lam_init = 0.8 - 0.6 * math.exp(-0.3 * l)
        mod = mod_all[l]
        w = w_in[l]
        w_main = jnp.concatenate([w[:, :GATE_OFF], w[:, GL_OFF:]], axis=1).astype(BF16)
        b_main = jnp.concatenate([b_in[l, :GATE_OFF], b_in[l, GL_OFF:]])[None, :]
        w_gate = jnp.pad(w[:, GATE_OFF:GL_OFF], ((0, 0), (0, LANE - 16))).astype(BF16)
        b_gate = jnp.pad(b_in[l, GATE_OFF:GL_OFF], (0, LANE - 16))[None, :]
        proj, gates = _inproj_call(x_pair, mod, norm1_g[l][None, :], w_main, b_main, w_gate, b_gate)

        g4 = gates[:, :16].reshape(T, 2, 2, M_HEADS)
        gates_col = jnp.transpose(g4, (3, 0, 1, 2)).reshape(M_HEADS, T, 4)
        gates_row = jnp.transpose(g4, (3, 1, 2, 0)).reshape(M_HEADS, 4, T)

        lamp = jnp.stack([lam_q1[l], lam_k1[l], lam_q2[l], lam_k2[l]])
        qg2 = jnp.tile(qn_g[l], 2)[None, :]
        kg2 = jnp.tile(kn_g[l], 2)[None, :]
        sg = subln_g[l][None, :]
        oa_p, *kv_cache = _attn_ctx_call(proj, lamp, qg2, kg2, sg, lam_init, l, tuple(kv_cache))
        oa_s = _attn_lat_call(proj, cache_k4, cache_v4, l, rope_cos, rope_sin, lamp, qg2, kg2, sg, lam_init)

        cw, cb, cg, cbeta = conv_w[l], conv_b[l][None, :], cln_g[l][None, :], cln_b[l][None, :]
        ob = _conv_call(proj, cw, cb, cg, cbeta)

        mg = mnorm_g[l][None, :]
        oc_p, c_new, nm_new = _mlstm_call(proj, gates_col, gates_row, mg, SEQ, BATCH, 0, l, carried=c_state)
        c_state = (c_new,)
        (oc_s,) = _mlstm_call(proj, gates_col, gates_row, mg, DEC_SEQ, DEC_BATCH, TP // DEC_SEQ, l,
                              states=(state_C, state_n, state_m))

        rw = jnp.pad(router_w[l], ((0, 0), (0, LANE - N_EXPERTS)))
        rwh = rw.astype(BF16)
        rwl = (rw - rwh.astype(F32)).astype(BF16)
        rb = jnp.pad(router_b[l], (0, LANE - N_EXPERTS))[None, :]
        x1, xn2, route, counts = _merge_call(
            (oa_p, oa_s), ob, (oc_p, oc_s), proj, x_pair, mod,
            w_br_a[l].astype(BF16), w_br_b[l].astype(BF16), w_br_c[l].astype(BF16), w_out[l].astype(BF16),
            norm2_g[l][None, :], rwh, rwl, rb)

        pos3, block_e, nused, valid = _routing_tables(route, counts)
        xs = _scatter_call(block_e, nused, pos3, xn2)
        ys = _gmm_call(block_e, valid, xs, w_gu, b_gu, w_dn, b_dn, l)
        if l < DEPTH - 1:
            x = _combine_call(pos3, ys, x1, route, mod)
            x_pair = (x, x)
        else:
            n_ctx = TP // TOK_TILE
            x_pair = (_combine_call(pos3, ys, x1, route, mod, 0, n_ctx),
                      _combine_call(pos3, ys, x1, route, mod, n_ctx, TS // TOK_TILE))

        ns.append(jnp.transpose(nm_new[..., :M_DIM], (0, 2, 1, 3)))
        ms.append(jnp.transpose(nm_new[..., M_DIM], (0, 2, 1)))

    y_p = x_pair[0].reshape(BATCH, SEQ, D)
    y_s = x_pair[1].reshape(DEC_BATCH, DEC_SEQ, D)
    new_k = kv_cache[0].reshape(BATCH, DEPTH, SEQ, A_HEADS, 2 * A_DIM)
    new_v = kv_cache[1].reshape(BATCH, DEPTH, SEQ, A_HEADS, A_VDIM)
    return (y_p, y_s, new_k, new_v, c_state[0], jnp.stack(ns, axis=1), jnp.stack(ms, axis=1))
```

```python
import functools
import math

import jax
import jax.numpy as jnp
import numpy as np
from jax import lax
from jax.experimental import pallas as pl
from jax.experimental.pallas import tpu as pltpu

F32 = jnp.float32
BF16 = jnp.bfloat16

D = 1024
BATCH = 32
SEQ = 256
DEPTH = 2
DEC_BATCH = 4
DEC_SEQ = 1024
PAST_LEN = 512
GRID_W = 64
EPS = 1e-6
A_HEADS = 8
A_DIM = 64
A_VDIM = 128
ROPE_BASE = 10000.0
CONV_WIDTH = 31
CONV_PAD = CONV_WIDTH // 2
M_HEADS = 4
M_DIM = 256
N_EXPERTS = 32
TOP_K = 4
SWIGLU_LIMIT = 7.0
SWIGLU_ALPHA = 1.702
N_MOD = 6

TP = BATCH * SEQ
TS = DEC_BATCH * DEC_SEQ
T = TP + TS
N_MAIN = 12 * D
GATE_OFF = 9 * D
GL_OFF = GATE_OFF + 16

LANE = 128
SUBLANE = 8
VMEM_LIMIT = 56 * 1024 * 1024

TM_IN = 1024
TN_IN = 1024
TM_MERGE = 512
TQ = 256
M_CHUNK = 256
CONV_TILE = 256
CONV_HALO = 16
CONV_ROWS = 32
CONV_COLS = 512
MOE_BM = 512
MOE_NB = T * TOP_K // MOE_BM + N_EXPERTS
MOE_ROWS = MOE_NB * MOE_BM
TOK_TILE = 256


def _sigmoid(x):
    return 1.0 / (1.0 + jnp.exp(-x))


def _log_sigmoid(x):
    return jnp.minimum(x, 0.0) - jnp.log1p(jnp.exp(-jnp.abs(x)))


def _dot(a, b):
    return jnp.dot(a, b, preferred_element_type=F32)


def _dot_nt(a, b):
    return lax.dot_general(a, b, (((1,), (1,)), ((), ())), preferred_element_type=F32)


def _dot_tn(a, b):
    return lax.dot_general(a, b, (((0,), (0,)), ((), ())), preferred_element_type=F32)


def _params(n_axes, vmem=None):
    return pltpu.CompilerParams(dimension_semantics=("arbitrary",) * n_axes,
                                vmem_limit_bytes=vmem)


def _ada_kernel(c_ref, w_ref, b_ref, o_ref):
    c = c_ref[...]
    s = (c * _sigmoid(c)).astype(BF16)
    o_ref[0] = _dot(s, w_ref[0].astype(BF16)) + b_ref[0]


def _ada_call(cond, ada_w, ada_b):
    n = N_MOD * D
    return pl.pallas_call(
        _ada_kernel,
        out_shape=jax.ShapeDtypeStruct((DEPTH, SUBLANE, n), F32),
        grid=(DEPTH, n // D),
        in_specs=[pl.BlockSpec((SUBLANE, D), lambda l, j: (0, 0)),
                  pl.BlockSpec((1, D, D), lambda l, j: (l, 0, j)),
                  pl.BlockSpec((1, 1, D), lambda l, j: (l, 0, j))],
        out_specs=pl.BlockSpec((1, SUBLANE, D), lambda l, j: (l, 0, j)),
        compiler_params=_params(2),
        name="ada_mod",
    )(cond, ada_w, ada_b.reshape(DEPTH, 1, n))


def _mod_row_in(i):
    return jnp.where(i < TP // TM_IN, 0, i - (TP // TM_IN - 1))


def _inproj_kernel(xp_ref, xs_ref, mod_ref, g_ref, w_ref, b_ref, wg_ref, bg_ref, o_ref, og_ref, xn_ref):
    @pl.when(pl.program_id(1) == 0)
    def _():
        x = jnp.where(pl.program_id(0) < TP // TM_IN, xp_ref[...], xs_ref[...])
        ms = jnp.mean(x * x, axis=-1, keepdims=True)
        y = x * lax.rsqrt(ms + EPS) * g_ref[...]
        xn = (y * (1.0 + mod_ref[0, 1:2, :]) + mod_ref[0, 0:1, :]).astype(BF16)
        xn_ref[...] = xn
        og_ref[...] = _dot(xn, wg_ref[...]) + bg_ref[...]

    o_ref[...] = (_dot(xn_ref[...], w_ref[...]) + b_ref[...]).astype(BF16)


def _x_specs(x_pair, tile):
    n_ctx = TP // tile
    off = n_ctx if x_pair[1].shape[0] == T else 0
    ctx = lambda i, *_: (jnp.minimum(i, n_ctx - 1), 0)
    lat = lambda i, *_: (off + jnp.maximum(i - n_ctx, 0), 0)
    return [pl.BlockSpec((tile, D), ctx), pl.BlockSpec((tile, D), lat)]


def _inproj_call(x_pair, mod, g, w_main, b_main, w_gate, b_gate):
    return pl.pallas_call(
        _inproj_kernel,
        out_shape=(jax.ShapeDtypeStruct((T, N_MAIN), BF16),
                   jax.ShapeDtypeStruct((T, LANE), F32)),
        grid=(T // TM_IN, N_MAIN // TN_IN),
        in_specs=_x_specs(x_pair, TM_IN) + [
                  pl.BlockSpec((1, N_MOD, D), lambda i, j: (_mod_row_in(i), 0, 0)),
                  pl.BlockSpec((1, D), lambda i, j: (0, 0)),
                  pl.BlockSpec((D, TN_IN), lambda i, j: (0, j)),
                  pl.BlockSpec((1, TN_IN), lambda i, j: (0, j)),
                  pl.BlockSpec((D, LANE), lambda i, j: (0, 0)),
                  pl.BlockSpec((1, LANE), lambda i, j: (0, 0))],
        out_specs=(pl.BlockSpec((TM_IN, TN_IN), lambda i, j: (i, j)),
                   pl.BlockSpec((TM_IN, LANE), lambda i, j: (i, 0))),
        scratch_shapes=[pltpu.VMEM((TM_IN, D), BF16)],
        compiler_params=_params(2, VMEM_LIMIT),
        name="in_proj",
    )(x_pair[0], x_pair[1], mod, g, w_main, b_main, w_gate, b_gate)


def _qk_norm(x, g2):
    lo = lax.broadcasted_iota(jnp.int32, x.shape, 1) < A_DIM
    x2 = x * x
    s0 = jnp.sum(jnp.where(lo, x2, 0.0), axis=-1, keepdims=True)
    s1 = jnp.sum(jnp.where(lo, 0.0, x2), axis=-1, keepdims=True)
    ms = jnp.where(lo, s0, s1) * (1.0 / A_DIM)
    return x * lax.rsqrt(ms + EPS) * g2


def _rope(x, cos, sin_signed):
    first = (lax.broadcasted_iota(jnp.int32, x.shape, 1) % 32) < 16
    partner = jnp.where(first, pltpu.roll(x, LANE - 16, 1), pltpu.roll(x, 16, 1))
    return x * cos + partner * sin_signed


def _lambda(lamp_ref, lam_init):
    l1 = jnp.sum(lamp_ref[0:1, :] * lamp_ref[1:2, :], axis=-1, keepdims=True)
    l2 = jnp.sum(lamp_ref[2:3, :] * lamp_ref[3:4, :], axis=-1, keepdims=True)
    return jnp.exp(l1) - jnp.exp(l2) + lam_init


def _softmax_times(s, vh):
    e = jnp.exp(s - jnp.max(s, axis=-1, keepdims=True))
    return _dot(e.astype(BF16), vh) / jnp.sum(e, axis=-1, keepdims=True)


def _softmax(s):
    e = jnp.exp(s - jnp.max(s, axis=-1, keepdims=True))
    return e / jnp.sum(e, axis=-1, keepdims=True)


def _diff_attn_head(qh, kh, vh, lam, sg, lam_init, split_pv):
    lo = lax.broadcasted_iota(jnp.int32, qh.shape, 1) < A_DIM
    qs = qh * (A_DIM ** -0.5)
    q0 = jnp.where(lo, qs, 0.0).astype(BF16)
    q1 = jnp.where(lo, 0.0, qs).astype(BF16)
    if split_pv:
        o = _softmax_times(_dot_nt(q0, kh), vh) - lam * _softmax_times(_dot_nt(q1, kh), vh)
    else:
        w = (_softmax(_dot_nt(q0, kh)) - lam * _softmax(_dot_nt(q1, kh))).astype(BF16)
        o = _dot(w, vh)
    ms = jnp.mean(o * o, axis=-1, keepdims=True)
    return o * lax.rsqrt(ms + EPS) * sg * (1.0 - lam_init)


def _attn_ctx_kernel(*refs, lam_init, n_carried):
    lamp_ref, q_ref, k_ref, v_ref, qg_ref, kg_ref, sg_ref = refs[:7]
    o_ref, kn_ref, vc_ref = refs[7 + n_carried:]
    lam = _lambda(lamp_ref, lam_init)
    n_slots = kn_ref.shape[1]
    for s in range(n_slots):
        vc_ref[0, s] = v_ref[...].astype(F32)
    for h in range(A_HEADS):
        sl = slice(h * LANE, (h + 1) * LANE)
        kn = _qk_norm(k_ref[:, sl].astype(F32), kg_ref[...])
        for s in range(n_slots):
            kn_ref[0, s, :, sl] = kn
        qn = _qk_norm(q_ref[:, sl].astype(F32), qg_ref[...])
        o = _diff_attn_head(qn, kn.astype(BF16), v_ref[:, sl], lam, sg_ref[...], lam_init, True)
        o_ref[:, sl] = o.astype(BF16)


def _attn_ctx_call(proj, lamp, qg2, kg2, sg, lam_init, layer, carried):
    blk = lambda c: pl.BlockSpec((SEQ, D), lambda b, c=c: (b, c))
    vec = pl.BlockSpec((1, LANE), lambda b: (0, 0))
    if carried:
        cache = pl.BlockSpec((1, 1, SEQ, D), lambda b: (b, layer, 0, 0))
    else:
        cache = pl.BlockSpec((1, DEPTH, SEQ, D), lambda b: (b, 0, 0, 0))
    cache_shape = jax.ShapeDtypeStruct((BATCH, DEPTH, SEQ, D), F32)
    n_in = 7
    return pl.pallas_call(
        functools.partial(_attn_ctx_kernel, lam_init=lam_init, n_carried=len(carried)),
        out_shape=(jax.ShapeDtypeStruct((TP, D), BF16), cache_shape, cache_shape),
        grid=(BATCH,),
        in_specs=[pl.BlockSpec((4, A_DIM), lambda b: (0, 0)), blk(0), blk(1), blk(2), vec, vec, vec]
                 + [pl.BlockSpec(memory_space=pl.ANY)] * len(carried),
        out_specs=(pl.BlockSpec((SEQ, D), lambda b: (b, 0)), cache, cache),
        input_output_aliases={n_in + j: 1 + j for j in range(len(carried))},
        compiler_params=_params(1, VMEM_LIMIT),
        name="attn_ctx",
    )(lamp, proj, proj, proj, qg2, kg2, sg, *carried)


def _attn_lat_kernel(lamp_ref, q_ref, k_ref, v_ref, kc_ref, vc_ref, cos_ref, sin_ref, cosq_ref, sinq_ref,
                     qg_ref, kg_ref, sg_ref, o_ref, kall_ref, vall_ref, *, lam_init):
    @pl.when(pl.program_id(1) == 0)
    def _():
        for h in range(A_HEADS):
            sl = slice(h * LANE, (h + 1) * LANE)
            kn = _rope(_qk_norm(k_ref[:, sl].astype(F32), kg_ref[...]), cos_ref[...], sin_ref[...])
            kall_ref[0:DEC_SEQ, sl] = kn.astype(BF16)
        vall_ref[0:DEC_SEQ, :] = v_ref[...]
        kall_ref[DEC_SEQ:, :] = kc_ref[0, 0].astype(BF16)
        vall_ref[DEC_SEQ:, :] = vc_ref[0, 0].astype(BF16)

    lam = _lambda(lamp_ref, lam_init)
    for h in range(A_HEADS):
        sl = slice(h * LANE, (h + 1) * LANE)
        qn = _rope(_qk_norm(q_ref[:, sl].astype(F32), qg_ref[...]), cosq_ref[...], sinq_ref[...])
        o = _diff_attn_head(qn, kall_ref[:, sl], vall_ref[:, sl], lam, sg_ref[...], lam_init, False)
        o_ref[:, sl] = o.astype(BF16)


def _attn_lat_call(proj, cache_k, cache_v, layer, rope_cos, rope_sin, lamp, qg2, kg2, sg, lam_init):
    nq = DEC_SEQ // TQ
    r0 = TP // DEC_SEQ
    vec = pl.BlockSpec((1, LANE), lambda b, i: (0, 0))
    ctx = pl.BlockSpec((1, 1, PAST_LEN, D), lambda b, i: (b, layer, 0, 0))
    full = lambda c: pl.BlockSpec((DEC_SEQ, D), lambda b, i, c=c: (r0 + b, c))
    return pl.pallas_call(
        functools.partial(_attn_lat_kernel, lam_init=lam_init),
        out_shape=jax.ShapeDtypeStruct((TS, D), BF16),
        grid=(DEC_BATCH, nq),
        in_specs=[pl.BlockSpec((4, A_DIM), lambda b, i: (0, 0)),
                  pl.BlockSpec((TQ, D), lambda b, i: ((TP // TQ) + b * nq + i, 0)),
                  full(1), full(2), ctx, ctx,
                  pl.BlockSpec((DEC_SEQ, LANE), lambda b, i: (0, 0)),
                  pl.BlockSpec((DEC_SEQ, LANE), lambda b, i: (0, 0)),
                  pl.BlockSpec((TQ, LANE), lambda b, i: (i, 0)),
                  pl.BlockSpec((TQ, LANE), lambda b, i: (i, 0)),
                  vec, vec, vec],
        out_specs=pl.BlockSpec((TQ, D), lambda b, i: (b * nq + i, 0)),
        scratch_shapes=[pltpu.VMEM((DEC_SEQ + PAST_LEN, D), BF16),
                        pltpu.VMEM((DEC_SEQ + PAST_LEN, D), BF16)],
        compiler_params=_params(2, VMEM_LIMIT),
        name="attn_lat",
    )(lamp, proj, proj, proj, cache_k, cache_v, rope_cos, rope_sin, rope_cos, rope_sin, qg2, kg2, sg)


def _rope_tables():
    t = np.arange(DEC_SEQ)
    row, col = t // GRID_W, t % GRID_W
    lane = np.arange(LANE)
    jj = lane % A_DIM
    freq = ROPE_BASE ** (-(jj % 16).astype(np.float64) / 16.0)
    pos = np.where((jj // 32)[None, :] == 0, row[:, None], col[:, None]).astype(np.float64)
    ang = pos.astype(np.float32) * freq.astype(np.float32)[None, :]
    sign = np.where((jj % 32) < 16, -1.0, 1.0).astype(np.float32)
    return jnp.cos(jnp.asarray(ang)), jnp.sin(jnp.asarray(ang)) * jnp.asarray(sign)[None, :]


def _conv_neighbours(i):
    j = (i - TP // CONV_TILE) % (DEC_SEQ // CONV_TILE)
    lat = i >= TP // CONV_TILE
    return lat & (j > 0), lat & (j < DEC_SEQ // CONV_TILE - 1)


def _conv_kernel(a_ref, gate_ref, ap_ref, gp_ref, an_ref, gn_ref, w_ref, b_ref, g_ref, beta_ref, o_ref,
                 hp_ref, hs_ref, acc_ref):
    halo = CONV_HALO
    seq = CONV_TILE
    has_prev, has_next = _conv_neighbours(pl.program_id(0))
    glu = lambda a, g: a[...].astype(F32) * _sigmoid(g[...].astype(F32))
    hp_ref[0:halo, :] = jnp.where(has_prev, glu(ap_ref, gp_ref), 0.0)
    hp_ref[halo + seq:, :] = jnp.where(has_next, glu(an_ref, gn_ref), 0.0)
    hp_ref[halo:halo + seq, :] = glu(a_ref, gate_ref)

    srows = hs_ref.shape[1]
    for s in range(SUBLANE):
        hs_ref[s] = hp_ref[s:s + srows, :]

    for t0 in range(0, seq, CONV_ROWS):
        for c0 in range(0, D, CONV_COLS):
            acc = jnp.zeros((CONV_ROWS, CONV_COLS), F32)
            for j in range(CONV_WIDTH):
                r = halo - CONV_PAD + t0 + j
                s = r % SUBLANE
                acc = acc + (w_ref[j:j + 1, c0:c0 + CONV_COLS]
                             * hs_ref[s, r - s:r - s + CONV_ROWS, c0:c0 + CONV_COLS])
            acc_ref[t0:t0 + CONV_ROWS, c0:c0 + CONV_COLS] = acc + b_ref[:, c0:c0 + CONV_COLS]

    h = acc_ref[...]
    mu = jnp.mean(h, axis=-1, keepdims=True)
    hc = h - mu
    y = hc * lax.rsqrt(jnp.mean(hc * hc, axis=-1, keepdims=True) + EPS) * g_ref[...] + beta_ref[...]
    o_ref[...] = (y * _sigmoid(y)).astype(BF16)


def _conv_call(proj, conv_w, conv_b, cln_g, cln_b):
    vec = pl.BlockSpec((1, D), lambda i: (0, 0))
    per = CONV_TILE // CONV_HALO
    cur = lambda c: pl.BlockSpec((CONV_TILE, D), lambda i, c=c: (i, c))
    prev = lambda c: pl.BlockSpec((CONV_HALO, D), lambda i, c=c: (jnp.maximum(i * per - 1, 0), c))
    nxt = lambda c: pl.BlockSpec((CONV_HALO, D), lambda i, c=c: (jnp.minimum((i + 1) * per, T // CONV_HALO - 1), c))
    return pl.pallas_call(
        _conv_kernel,
        out_shape=jax.ShapeDtypeStruct((T, D), BF16),
        grid=(T // CONV_TILE,),
        in_specs=[cur(3), cur(4), prev(3), prev(4), nxt(3), nxt(4),
                  pl.BlockSpec((CONV_WIDTH, D), lambda i: (0, 0)),
                  vec, vec, vec],
        out_specs=pl.BlockSpec((CONV_TILE, D), lambda i: (i, 0)),
        scratch_shapes=[pltpu.VMEM((CONV_TILE + 2 * CONV_HALO, D), F32),
                        pltpu.VMEM((SUBLANE, CONV_TILE + 2 * CONV_HALO - SUBLANE, D), F32),
                        pltpu.VMEM((CONV_TILE, D), F32)],
        compiler_params=_params(1, VMEM_LIMIT),
        name="conformer_conv",
    )(proj, proj, proj, proj, proj, proj, conv_w, conv_b, cln_g, cln_b)


def _mlstm_kernel(*refs, seq, has_state, emit_state, layer, n_carried):
    if has_state:
        (m0_ref, q_ref, k_ref, v_ref, og_ref, gc_ref, gr_ref, mg_ref, c0_ref, n0_ref) = refs[:10]
        rest = refs[10:]
    else:
        (q_ref, k_ref, v_ref, og_ref, gc_ref, gr_ref, mg_ref) = refs[:7]
        rest = refs[7 + n_carried:]
    if emit_state:
        o_ref, cout_ref, nm_ref, hacc_ref, c_ref = rest
    else:
        o_ref, hacc_ref, c_ref = rest

    b_id = pl.program_id(0)
    h_id = pl.program_id(1)
    L = M_CHUNK
    nc = seq // L
    t_idx = lax.broadcasted_iota(jnp.int32, (L, L), 0)
    s_idx = lax.broadcasted_iota(jnp.int32, (L, L), 1)

    for d in range(2):
        mask = (s_idx <= t_idx) if d == 0 else (s_idx >= t_idx)
        mask_t = (t_idx <= s_idx) if d == 0 else (t_idx >= s_idx)
        last = L - 1 if d == 0 else 0
        if has_state:
            c_ref[...] = c0_ref[0, 0, d, 0]
            n = n0_ref[0, 0, d, pl.ds(h_id, 1), :]
            m = jnp.full((1, 1), m0_ref[b_id, layer, d, h_id], F32)
        else:
            n = jnp.zeros((1, M_DIM), F32)
            m = jnp.zeros((1, 1), F32)
        chunks = range(nc) if d == 0 else range(nc - 1, -1, -1)
        for ci, c in enumerate(chunks):
            rows = slice(c * L, (c + 1) * L)
            first_chunk = ci == 0
            last_chunk = ci == nc - 1
            qb = q_ref[rows, :]
            q = qb.astype(F32)
            k = k_ref[rows, :].astype(F32) * (M_DIM ** -0.5)
            kb, vb = k.astype(BF16), v_ref[rows, :]
            gi_col = gc_ref[0, rows, d:d + 1]
            gi_row = gr_ref[0, d:d + 1, rows]
            lf_col = _log_sigmoid(gc_ref[0, rows, 2 + d:3 + d])
            lf_row = _log_sigmoid(gr_ref[0, 2 + d:3 + d, rows])
            b_col = jnp.sum(jnp.where(mask, lf_row, 0.0), axis=1, keepdims=True)
            b_row = jnp.sum(jnp.where(mask_t, lf_col, 0.0), axis=0, keepdims=True)
            dmat = jnp.where(mask, b_col - b_row + gi_row, -jnp.inf)
            inter = b_col + m
            mt = jnp.maximum(inter, jnp.max(dmat, axis=1, keepdims=True))
            w = jnp.exp(dmat - mt)
            s_inter = jnp.exp(inter - mt)
            sw = _dot_nt(qb, kb) * w
            num = _dot(sw.astype(BF16), vb)
            den = jnp.sum(sw, axis=1, keepdims=True)
            if has_state or not first_chunk:
                num = num + s_inter * _dot(qb, c_ref[...].astype(BF16))
                den = den + s_inter * jnp.sum(q * n, axis=1, keepdims=True)
            hout = num / jnp.maximum(jnp.abs(den), jnp.exp(-mt))
            if d == 0:
                hacc_ref[rows, :] = hout
            else:
                hacc_ref[rows, :] = hacc_ref[rows, :] + hout
            if emit_state or not last_chunk:
                m_new = mt[last:last + 1, :]
                btot = b_col[last:last + 1, :]
                decay = jnp.exp(btot + m - m_new)
                wk_col = jnp.exp(btot - b_col + gi_col - m_new)
                kw = k * wk_col
                upd = _dot_tn(kw.astype(BF16), vb)
                nsum = jnp.sum(kw, axis=0, keepdims=True)
                if has_state or not first_chunk:
                    c_ref[...] = decay * c_ref[...] + upd
                    n = decay * n + nsum
                else:
                    c_ref[...] = upd
                    n = nsum
                m = m_new
        if emit_state:
            for s in range(cout_ref.shape[1]):
                cout_ref[0, s, d, 0] = c_ref[...]
            nm_ref[0, 0, d:d + 1, 0:M_DIM] = n
            nm_ref[0, 0, d:d + 1, M_DIM:] = jnp.broadcast_to(m, (1, LANE))

    hm = hacc_ref[...]
    ms = jnp.mean(hm * hm, axis=-1, keepdims=True)
    y = hm * lax.rsqrt(ms + EPS) * mg_ref[...]
    o_ref[...] = (y * _sigmoid(og_ref[...].astype(F32))).astype(BF16)


def _mlstm_call(proj, gates_col, gates_row, mnorm_g, seq, nb, row0, layer, states=None, carried=()):
    has_state = states is not None
    emit_state = not has_state
    aliases = {}
    col = lambda c0: pl.BlockSpec((seq, M_DIM), lambda b, h, c0=c0: (row0 + b, c0 + h))
    in_specs = [col(20), col(24), col(28), col(32),
                pl.BlockSpec((1, seq, 4), lambda b, h: (h, row0 + b, 0)),
                pl.BlockSpec((1, 4, seq), lambda b, h: (h, 0, row0 + b)),
                pl.BlockSpec((1, M_DIM), lambda b, h: (0, h))]
    args = [proj, proj, proj, proj, gates_col, gates_row, mnorm_g]
    if has_state:
        state_c, state_n, state_m = states
        in_specs = [pl.BlockSpec(memory_space=pltpu.SMEM)] + in_specs + [
            pl.BlockSpec((1, 1, 2, 1, M_DIM, M_DIM), lambda b, h: (b, layer, 0, h, 0, 0)),
            pl.BlockSpec((1, 1, 2, M_HEADS, M_DIM), lambda b, h: (b, layer, 0, 0, 0))]
        args = [state_m] + args + [state_c, state_n]
    out_shape = [jax.ShapeDtypeStruct((nb * seq, D), BF16)]
    out_specs = [pl.BlockSpec((seq, M_DIM), lambda b, h: (b, h))]
    if emit_state:
        out_shape += [jax.ShapeDtypeStruct((nb, DEPTH, 2, M_HEADS, M_DIM, M_DIM), F32),
                      jax.ShapeDtypeStruct((nb, M_HEADS, 2, M_DIM + LANE), F32)]
        if carried:
            c_spec = pl.BlockSpec((1, 1, 2, 1, M_DIM, M_DIM), lambda b, h: (b, layer, 0, h, 0, 0))
        else:
            c_spec = pl.BlockSpec((1, DEPTH, 2, 1, M_DIM, M_DIM), lambda b, h: (b, 0, 0, h, 0, 0))
        out_specs += [c_spec, pl.BlockSpec((1, 1, 2, M_DIM + LANE), lambda b, h: (b, h, 0, 0))]
        aliases = {len(args) + j: 1 + j for j in range(len(carried))}
        in_specs = in_specs + [pl.BlockSpec(memory_space=pl.ANY)] * len(carried)
        args = args + list(carried)
    return pl.pallas_call(
        functools.partial(_mlstm_kernel, seq=seq, has_state=has_state, emit_state=emit_state, layer=layer,
                          n_carried=len(carried)),
        out_shape=tuple(out_shape),
        grid=(nb, M_HEADS),
        in_specs=in_specs,
        out_specs=tuple(out_specs),
        input_output_aliases=aliases,
        scratch_shapes=[pltpu.VMEM((seq, M_DIM), F32), pltpu.VMEM((M_DIM, M_DIM), F32)],
        compiler_params=_params(2, VMEM_LIMIT),
        name="mlstm",
    )(*args)


def _mod_row_merge(i):
    per = DEC_SEQ // TM_MERGE
    return jnp.where(i < TP // TM_MERGE, 0, 1 + (i - TP // TM_MERGE) // per)


def _split3(x):
    hi = x.astype(BF16)
    lo = (x - hi.astype(F32)).astype(BF16)
    return hi, lo


def _merge_kernel(oap, oas, ob_ref, ocp, ocs, gl_ref, xp_ref, xs_ref, mod_ref, wa_ref, wb_ref, wc_ref, wo_ref,
                  g2_ref, rwh_ref, rwl_ref, rb_ref, x1_ref, xn_ref, route_ref, cnt_ref, carry_ref):
    i = pl.program_id(0)
    is_ctx = i < TP // TM_MERGE

    @pl.when(i == 0)
    def _():
        carry_ref[...] = jnp.zeros_like(carry_ref)

    def branch(p_ref, s_ref, w_ref, c):
        o = jnp.where(is_ctx, p_ref[...], s_ref[...])
        return _sigmoid(gl_ref[:, c * D:(c + 1) * D].astype(F32)) * _dot(o, w_ref[...])

    merged = (branch(oap, oas, wa_ref, 0)
              + _sigmoid(gl_ref[:, D:2 * D].astype(F32)) * _dot(ob_ref[...], wb_ref[...])
              + branch(ocp, ocs, wc_ref, 2))
    out = _dot(merged.astype(BF16), wo_ref[...])
    x1 = jnp.where(is_ctx, xp_ref[...], xs_ref[...]) + mod_ref[0, 2:3, :] * out
    x1_ref[...] = x1
    ms = jnp.mean(x1 * x1, axis=-1, keepdims=True)
    xn = x1 * lax.rsqrt(ms + EPS) * g2_ref[...] * (1.0 + mod_ref[0, 4:5, :]) + mod_ref[0, 3:4, :]
    xn_ref[...] = xn

    xh, xl = _split3(xn)
    logits = _dot(xh, rwh_ref[...]) + _dot(xh, rwl_ref[...]) + _dot(xl, rwh_ref[...]) + rb_ref[...]
    lane = lax.broadcasted_iota(jnp.int32, logits.shape, 1)
    lanef = lane.astype(F32)
    cur = jnp.where(lane < N_EXPERTS, logits, -jnp.inf)
    vals, idxs = [], []
    for _ in range(TOP_K):
        mx = jnp.max(cur, axis=-1, keepdims=True)
        idx = jnp.min(jnp.where(cur == mx, lanef, float(LANE)), axis=-1, keepdims=True)
        vals.append(mx)
        idxs.append(idx)
        cur = jnp.where(lanef == idx, -jnp.inf, cur)
    exps = [jnp.exp(v - vals[0]) for v in vals]
    den = exps[0] + exps[1] + exps[2] + exps[3]

    onehot = jnp.zeros(logits.shape, F32)
    for idx in idxs:
        onehot = onehot + jnp.where(lanef == idx, 1.0, 0.0)
    r_idx = lax.broadcasted_iota(jnp.int32, (TM_MERGE, TM_MERGE), 0)
    c_idx = lax.broadcasted_iota(jnp.int32, (TM_MERGE, TM_MERGE), 1)
    tril = jnp.where(c_idx <= r_idx, 1.0, 0.0).astype(BF16)
    incl = _dot(tril, onehot.astype(BF16))
    tot = carry_ref[...] + incl
    route = jnp.zeros(logits.shape, F32)
    for k in range(TOP_K):
        rank = jnp.sum(jnp.where(lanef == idxs[k], tot, 0.0), axis=-1, keepdims=True) - 1.0
        route = jnp.where(lane == k, idxs[k], route)
        route = jnp.where(lane == TOP_K + k, exps[k] / den, route)
        route = jnp.where(lane == 2 * TOP_K + k, rank, route)
    route_ref[...] = route
    carry_ref[...] = tot[TM_MERGE - 1:TM_MERGE, :]
    cnt_ref[...] = jnp.broadcast_to(tot[TM_MERGE - 1:TM_MERGE, :], cnt_ref.shape)


def _merge_call(oa, ob, oc, proj, x_pair, mod, wa, wb, wc, wo, g2, rwh, rwl, rb):
    npt = TP // TM_MERGE
    pblk = pl.BlockSpec((TM_MERGE, D), lambda i: (jnp.minimum(i, npt - 1), 0))
    sblk = pl.BlockSpec((TM_MERGE, D), lambda i: (jnp.maximum(i - npt, 0), 0))
    wblk = pl.BlockSpec((D, D), lambda i: (0, 0))
    rblk = pl.BlockSpec((D, LANE), lambda i: (0, 0))
    return pl.pallas_call(
        _merge_kernel,
        out_shape=(jax.ShapeDtypeStruct((T, D), F32),
                   jax.ShapeDtypeStruct((T, D), F32),
                   jax.ShapeDtypeStruct((T, LANE), F32),
                   jax.ShapeDtypeStruct((SUBLANE, LANE), F32)),
        grid=(T // TM_MERGE,),
        in_specs=[pblk, sblk, pl.BlockSpec((TM_MERGE, D), lambda i: (i, 0)), pblk, sblk,
                  pl.BlockSpec((TM_MERGE, 3 * D), lambda i: (i, 3))]
                 + _x_specs(x_pair, TM_MERGE) + [
                  pl.BlockSpec((1, N_MOD, D), lambda i: (_mod_row_merge(i), 0, 0)),
                  wblk, wblk, wblk, wblk,
                  pl.BlockSpec((1, D), lambda i: (0, 0)),
                  rblk, rblk,
                  pl.BlockSpec((1, LANE), lambda i: (0, 0))],
        out_specs=(pl.BlockSpec((TM_MERGE, D), lambda i: (i, 0)),
                   pl.BlockSpec((TM_MERGE, D), lambda i: (i, 0)),
                   pl.BlockSpec((TM_MERGE, LANE), lambda i: (i, 0)),
                   pl.BlockSpec((SUBLANE, LANE), lambda i: (0, 0))),
        scratch_shapes=[pltpu.VMEM((1, LANE), F32)],
        compiler_params=_params(1, VMEM_LIMIT),
        name="merge_router",
    )(oa[0], oa[1], ob, oc[0], oc[1], proj, x_pair[0], x_pair[1], mod, wa, wb, wc, wo, g2, rwh, rwl, rb)


def _row_copy(src_ref, src_row, dst_ref, dst_row, sem):
    return pltpu.make_async_copy(src_ref.at[pl.ds(src_row, 1), :], dst_ref.at[pl.ds(dst_row, 1), :], sem)


def _tile_copy(src_ref, dst_ref, sem):
    return pltpu.make_async_copy(src_ref, dst_ref, sem)


def _scatter_kernel(be_ref, nused_ref, pos_ref, x_ref, xs_ref, zeros_ref, sem):
    i = pl.program_id(0)

    @pl.when(i == 0)
    def _():
        zeros_ref[...] = jnp.zeros_like(zeros_ref)

        def partial_block(b):
            nxt = be_ref[jnp.minimum(b + 1, MOE_NB - 1)]
            return (b >= nused_ref[0] - 1) | (nxt != be_ref[b])

        def fill(b, carry):
            @pl.when(partial_block(b))
            def _():
                pltpu.make_async_copy(zeros_ref, xs_ref.at[pl.ds(b * MOE_BM, MOE_BM), :], sem).start()
            return carry

        def fill_wait(b, carry):
            @pl.when(partial_block(b))
            def _():
                pltpu.make_async_copy(zeros_ref, xs_ref.at[pl.ds(b * MOE_BM, MOE_BM), :], sem).wait()
            return carry

        lax.fori_loop(0, MOE_NB, fill, 0)
        lax.fori_loop(0, MOE_NB, fill_wait, 0)

    def start(r, carry):
        for k in range(TOP_K):
            _row_copy(x_ref, r, xs_ref, pos_ref[0, 0, r * TOP_K + k], sem).start()
        return carry

    lax.fori_loop(0, TOK_TILE, start, 0)
    for k in range(TOP_K):
        _tile_copy(x_ref, xs_ref.at[pl.ds(0, TOK_TILE), :], sem).wait()


def _scatter_call(block_e, nused, pos3, xn):
    grid_spec = pltpu.PrefetchScalarGridSpec(
        num_scalar_prefetch=2,
        grid=(T // TOK_TILE,),
        in_specs=[pl.BlockSpec((1, 1, TOK_TILE * TOP_K), lambda i, be, nu: (i, 0, 0), memory_space=pltpu.SMEM),
                  pl.BlockSpec((TOK_TILE, D), lambda i, be, nu: (i, 0))],
        out_specs=pl.BlockSpec(memory_space=pl.ANY),
        scratch_shapes=[pltpu.VMEM((MOE_BM, D), F32), pltpu.SemaphoreType.DMA],
    )
    return pl.pallas_call(
        _scatter_kernel,
        out_shape=jax.ShapeDtypeStruct((MOE_ROWS, D), F32),
        grid_spec=grid_spec,
        compiler_params=_params(1),
        name="moe_scatter",
    )(block_e, nused, pos3, xn)


def _gmm_kernel(be_ref, valid_ref, x_ref, wgu_ref, bgu_ref, wdn_ref, bdn_ref, y_ref, wgu_bf, wdn_bf):
    i = pl.program_id(0)
    prev = be_ref[jnp.maximum(i - 1, 0)]
    valid = valid_ref[i]

    @pl.when((i == 0) | (be_ref[i] != prev))
    def _():
        wgu_bf[...] = wgu_ref[0, 0].astype(BF16)
        wdn_bf[...] = wdn_ref[0, 0].astype(BF16)

    def expert_mlp(rows):
        gu = _dot(x_ref[0:rows, :].astype(BF16), wgu_bf[...]) + bgu_ref[0, 0]
        g = jnp.minimum(gu[:, :D], SWIGLU_LIMIT)
        u = jnp.clip(gu[:, D:], -SWIGLU_LIMIT, SWIGLU_LIMIT)
        hdn = (u + 1.0) * (g * _sigmoid(SWIGLU_ALPHA * g))
        y_ref[0:rows, :] = _dot(hdn.astype(BF16), wdn_bf[...]) + bdn_ref[0, 0]
        if rows < MOE_BM:
            y_ref[rows:, :] = jnp.zeros((MOE_BM - rows, D), F32)

    @pl.when(valid > MOE_BM // 2)
    def _():
        expert_mlp(MOE_BM)

    @pl.when((valid > 0) & (valid <= MOE_BM // 2))
    def _():
        expert_mlp(MOE_BM // 2)

    @pl.when(valid == 0)
    def _():
        y_ref[...] = jnp.zeros_like(y_ref)


def _gmm_call(block_e, valid, xs, w_gu, b_gu, w_dn, b_dn, layer):
    grid_spec = pltpu.PrefetchScalarGridSpec(
        num_scalar_prefetch=2,
        grid=(MOE_NB,),
        in_specs=[pl.BlockSpec((MOE_BM, D), lambda i, be, nu: (i, 0)),
                  pl.BlockSpec((1, 1, D, 2 * D), lambda i, be, nu: (layer, be[i], 0, 0)),
                  pl.BlockSpec((1, 1, 1, 2 * D), lambda i, be, nu: (layer, be[i], 0, 0)),
                  pl.BlockSpec((1, 1, D, D), lambda i, be, nu: (layer, be[i], 0, 0)),
                  pl.BlockSpec((1, 1, 1, D), lambda i, be, nu: (layer, be[i], 0, 0))],
        out_specs=pl.BlockSpec((MOE_BM, D), lambda i, be, nu: (i, 0)),
        scratch_shapes=[pltpu.VMEM((D, 2 * D), BF16), pltpu.VMEM((D, D), BF16)],
    )
    return pl.pallas_call(
        _gmm_kernel,
        out_shape=jax.ShapeDtypeStruct((MOE_ROWS, D), F32),
        grid_spec=grid_spec,
        compiler_params=_params(1, VMEM_LIMIT),
        name="moe_experts",
    )(block_e, valid, xs, w_gu, b_gu.reshape(DEPTH, N_EXPERTS, 1, 2 * D), w_dn,
      b_dn.reshape(DEPTH, N_EXPERTS, 1, D))


def _mod_row_tok(i):
    per = DEC_SEQ // TOK_TILE
    return jnp.where(i < TP // TOK_TILE, 0, 1 + (i - TP // TOK_TILE) // per)


def _combine_kernel(pos_ref, pos_next_ref, ys_ref, x1_ref, route_ref, mod_ref, o_ref, buf_ref, sems, *, n_tiles):
    i = pl.program_id(0)
    last = n_tiles - 1
    slot = i % 2

    def gather(p_ref, s):
        def start(r, carry):
            for k in range(TOP_K):
                _row_copy(ys_ref, p_ref[0, 0, r * TOP_K + k], buf_ref.at[s, k], r, sems.at[s]).start()
            return carry

        lax.fori_loop(0, TOK_TILE, start, 0)

    @pl.when(i == 0)
    def _():
        gather(pos_ref, slot)

    @pl.when(i < last)
    def _():
        gather(pos_next_ref, 1 - slot)

    for k in range(TOP_K):
        _tile_copy(ys_ref.at[pl.ds(0, TOK_TILE), :], buf_ref.at[slot, k], sems.at[slot]).wait()

    acc = jnp.zeros((TOK_TILE, D), F32)
    for k in range(TOP_K):
        acc = acc + route_ref[:, TOP_K + k:TOP_K + k + 1] * buf_ref[slot, k]
    o_ref[...] = x1_ref[...] + mod_ref[0, 5:6, :] * acc


def _combine_call(pos3, ys, x1, route, mod, tile0=0, n_tiles=T // TOK_TILE):
    pos_spec = lambda f: pl.BlockSpec((1, 1, TOK_TILE * TOP_K), lambda i: (tile0 + f(i), 0, 0),
                                      memory_space=pltpu.SMEM)
    return pl.pallas_call(
        functools.partial(_combine_kernel, n_tiles=n_tiles),
        out_shape=jax.ShapeDtypeStruct((n_tiles * TOK_TILE, D), F32),
        grid=(n_tiles,),
        in_specs=[pos_spec(lambda i: i), pos_spec(lambda i: jnp.minimum(i + 1, n_tiles - 1)),
                  pl.BlockSpec(memory_space=pl.ANY),
                  pl.BlockSpec((TOK_TILE, D), lambda i: (tile0 + i, 0)),
                  pl.BlockSpec((TOK_TILE, LANE), lambda i: (tile0 + i, 0)),
                  pl.BlockSpec((1, N_MOD, D), lambda i: (_mod_row_tok(tile0 + i), 0, 0))],
        out_specs=pl.BlockSpec((TOK_TILE, D), lambda i: (i, 0)),
        scratch_shapes=[pltpu.VMEM((2, TOP_K, TOK_TILE, D), F32), pltpu.SemaphoreType.DMA((2,))],
        compiler_params=_params(1, VMEM_LIMIT),
        name="moe_combine",
    )(pos3, pos3, ys, x1, route, mod)


def _pos_kernel(route_ref, start_ref, o_ref):
    r = route_ref[...]
    lane = lax.broadcasted_iota(jnp.int32, r.shape, 1)
    lanef = lane.astype(F32)
    out = jnp.zeros(r.shape, jnp.int32)
    for k in range(TOP_K):
        first = jnp.sum(jnp.where(lanef == r[:, k:k + 1], start_ref[...], 0.0), axis=-1, keepdims=True)
        pos = first + r[:, 2 * TOP_K + k:2 * TOP_K + k + 1]
        out = jnp.where(lane == k, pos.astype(jnp.int32), out)
    o_ref[...] = out


def _pos_call(route, pad_start):
    return pl.pallas_call(
        _pos_kernel,
        out_shape=jax.ShapeDtypeStruct((T, LANE), jnp.int32),
        grid=(T // TM_IN,),
        in_specs=[pl.BlockSpec((TM_IN, LANE), lambda i: (i, 0)),
                  pl.BlockSpec((1, LANE), lambda i: (0, 0))],
        out_specs=pl.BlockSpec((TM_IN, LANE), lambda i: (i, 0)),
        compiler_params=_params(1),
        name="moe_rows",
    )(route, pad_start)


def _routing_tables(route, counts):
    cnt = counts[0, :N_EXPERTS].astype(jnp.int32)
    padded = (cnt + MOE_BM - 1) // MOE_BM * MOE_BM
    pad_end = jnp.cumsum(padded)
    pad_start = pad_end - padded
    starts = jnp.arange(MOE_NB, dtype=jnp.int32) * MOE_BM
    block_e = jnp.minimum(jnp.sum((pad_end[None, :] <= starts[:, None]).astype(jnp.int32), axis=1),
                          N_EXPERTS - 1)
    valid = jnp.clip((pad_start + cnt)[block_e] - starts, 0, MOE_BM)
    nused = (pad_end[-1:] // MOE_BM).astype(jnp.int32)
    start_row = jnp.pad(pad_start.astype(F32), (0, LANE - N_EXPERTS))[None, :]
    pos = _pos_call(route, start_row)[:, :TOP_K]
    return pos.reshape(T // TOK_TILE, 1, TOK_TILE * TOP_K), block_e, nused, valid


def kernel(x_prompt, x_sample, c, c_ctx, cache_k, cache_v, state_C, state_n, state_m, ada_w, ada_b, norm1_g, norm2_g, w_in, b_in, qn_g, kn_g, lam_q1, lam_k1, lam_q2, lam_k2, subln_g, conv_w, conv_b, cln_g, cln_b, mnorm_g, w_br_a, w_br_b, w_br_c, w_out, router_w, router_b, w_gu, b_gu, w_dn, b_dn):
    x_pair = (x_prompt.reshape(TP, D), x_sample.reshape(TS, D))
    cond = jnp.concatenate([c_ctx[None, :], c, jnp.zeros((SUBLANE - 1 - DEC_BATCH, D), F32)], axis=0)
    mod_all = _ada_call(cond, ada_w, ada_b).reshape(DEPTH, SUBLANE, N_MOD, D)
    rope_cos, rope_sin = _rope_tables()
    cache_k4 = cache_k.reshape(DEC_BATCH, DEPTH, PAST_LEN, D)
    cache_v4 = cache_v.reshape(DEC_BATCH, DEPTH, PAST_LEN, D)

    kv_cache, c_state, ns, ms = (), (), [], []
    for l in range(DEPTH):
        lam_init = 0.8 - 0.6 * math.exp(-0.3 * l)
        mod = mod_all[l]
        w = w_in[l]
        w_main = jnp.concatenate([w[:, :GATE_OFF], w[:, GL_OFF:]], axis=1).astype(BF16)
        b_main = jnp.concatenate([b_in[l, :GATE_OFF], b_in[l, GL_OFF:]])[None, :]
        w_gate = jnp.pad(w[:, GATE_OFF:GL_OFF], ((0, 0), (0, LANE - 16))).astype(BF16)
        b_gate = jnp.pad(b_in[l, GATE_OFF:GL_OFF], (0, LANE - 16))[None, :]
        proj, gates = _inproj_call(x_pair, mod, norm1_g[l][None, :], w_main, b_main, w_gate, b_gate)

        g4 = gates[:, :16].reshape(T, 2, 2, M_HEADS)
        gates_col = jnp.transpose(g4, (3, 0, 1, 2)).reshape(M_HEADS, T, 4)
        gates_row = jnp.transpose(g4, (3, 1, 2, 0)).reshape(M_HEADS, 4, T)

        lamp = jnp.stack([lam_q1[l], lam_k1[l], lam_q2[l], lam_k2[l]])
        qg2 = jnp.tile(qn_g[l], 2)[None, :]
        kg2 = jnp.tile(kn_g[l], 2)[None, :]
        sg = subln_g[l][None, :]
        oa_p, *kv_cache = _attn_ctx_call(proj, lamp, qg2, kg2, sg, lam_init, l, tuple(kv_cache))
        oa_s = _attn_lat_call(proj, cache_k4, cache_v4, l, rope_cos, rope_sin, lamp, qg2, kg2, sg, lam_init)

        cw, cb, cg, cbeta = conv_w[l], conv_b[l][None, :], cln_g[l][None, :], cln_b[l][None, :]
        ob = _conv_call(proj, cw, cb, cg, cbeta)

        mg = mnorm_g[l][None, :]
        oc_p, c_new, nm_new = _mlstm_call(proj, gates_col, gates_row, mg, SEQ, BATCH, 0, l, carried=c_state)
        c_state = (c_new,)
        (oc_s,) = _mlstm_call(proj, gates_col, gates_row, mg, DEC_SEQ, DEC_BATCH, TP // DEC_SEQ, l,
                              states=(state_C, state_n, state_m))

        rw = jnp.pad(router_w[l], ((0, 0), (0, LANE - N_EXPERTS)))
        rwh = rw.astype(BF16)
        rwl = (rw - rwh.astype(F32)).astype(BF16)
        rb = jnp.pad(router_b[l], (0, LANE - N_EXPERTS))[None, :]
        x1, xn2, route, counts = _merge_call(
            (oa_p, oa_s), ob, (oc_p, oc_s), proj, x_pair, mod,
            w_br_a[l].astype(BF16), w_br_b[l].astype(BF16), w_br_c[l].astype(BF16), w_out[l].astype(BF16),
            norm2_g[l][None, :], rwh, rwl, rb)

        pos3, block_e, nused, valid = _routing_tables(route, counts)
        xs = _scatter_call(block_e, nused, pos3, xn2)
        ys = _gmm_call(block_e, valid, xs, w_gu, b_gu, w_dn, b_dn, l)
        if l < DEPTH - 1:
            x = _combine_call(pos3, ys, x1, route, mod)
            x_pair = (x, x)
        else:
            n_ctx = TP // TOK_TILE
            x_pair = (_combine_call(pos3, ys, x1, route, mod, 0, n_ctx),
                      _combine_call(pos3, ys, x1, route, mod, n_ctx, TS // TOK_TILE))

        ns.append(jnp.transpose(nm_new[..., :M_DIM], (0, 2, 1, 3)))
        ms.append(jnp.transpose(nm_new[..., M_DIM], (0, 2, 1)))

    y_p = x_pair[0].reshape(BATCH, SEQ, D)
    y_s = x_pair[1].reshape(DEC_BATCH, DEC_SEQ, D)
    new_k = kv_cache[0].reshape(BATCH, DEPTH, SEQ, A_HEADS, 2 * A_DIM)
    new_v = kv_cache[1].reshape(BATCH, DEPTH, SEQ, A_HEADS, A_VDIM)
    return (y_p, y_s, new_k, new_v, c_state[0], jnp.stack(ns, axis=1), jnp.stack(ms, axis=1))
```
